```python
import jax, jax.numpy as jnp
from jax import lax
import numpy as np

D_MODEL = 1024
BATCH = 8
SEQ = 2048
DEPTH = 1
DEC_BATCH = 128
DEC_SEQ = 1
PAST_LEN = 8192
PAGE_SIZE = 128

MIX_WIDTH = D_MODEL
POOL_WIDTH = MIX_WIDTH // 4
POOL_WINDOWS = (2, 4, 8, 16)
POOL_GROUPS = len(POOL_WINDOWS)
POOL_GROUP_DIM = POOL_WIDTH // POOL_GROUPS
POOL_HIST = max(POOL_WINDOWS) - 1
HEAD_DIM = 64
N_HEADS = (MIX_WIDTH - POOL_WIDTH) // HEAD_DIM
N_KV_HEADS = 4
GQA = N_HEADS // N_KV_HEADS
NSA_WIDTH = N_HEADS * HEAD_DIM
KV_WIDTH = N_KV_HEADS * HEAD_DIM
CMP_BLOCK = 32
CMP_STRIDE = 16
SEL_BLOCK = 64
SEL_TOP_N = 16
WINDOW = 512
ROPE_DIM = HEAD_DIM // 4
ROPE_THETA = 500000.0
Q_CHUNK = 64
N_BRANCH = 3
N_CACHED = 4
N_WIN = 2
EPS = 1e-6
FORCE_SCORE = 1e4
IN_SPLITS = (POOL_WIDTH, POOL_WIDTH, NSA_WIDTH, NSA_WIDTH, N_CACHED * KV_WIDTH, N_WIN * KV_WIDTH, N_HEADS * N_BRANCH)
IN_WIDTH = sum(IN_SPLITS)

kernel_name = 'hymba_pool_nsa_decode_step'


def rms_norm(x, gain):
    xf = x.astype(jnp.float32)
    y = xf * lax.rsqrt(jnp.mean(xf * xf, axis=-1, keepdims=True) + EPS)
    return (y * gain.astype(jnp.float32)).astype(x.dtype)


def partial_rope(x, pos):
    half = ROPE_DIM // 2
    freqs = ROPE_THETA ** (-jnp.arange(half, dtype=jnp.float32) / half)
    ang = pos.astype(jnp.float32)[:, None] * freqs[None, :]
    cos = jnp.cos(ang)[None, :, None, :]
    sin = jnp.sin(ang)[None, :, None, :]
    xf = x.astype(jnp.float32)
    x1 = xf[..., :half]
    x2 = xf[..., half:ROPE_DIM]
    out = jnp.concatenate([x1 * cos - x2 * sin, x2 * cos + x1 * sin, xf[..., ROPE_DIM:]], axis=-1)
    return out.astype(x.dtype)


def masked_softmax(s, mask):
    s = jnp.where(mask, s.astype(jnp.float32), -jnp.inf)
    m = jnp.max(s, axis=-1, keepdims=True)
    m = jnp.where(jnp.isfinite(m), m, 0.0)
    e = jnp.exp(s - m)
    return e / jnp.maximum(jnp.sum(e, axis=-1, keepdims=True), 1e-30)


def project_tokens(x, pos, ln_gain, w_in, q_norm, k_norm):
    B, L, _ = x.shape
    h = rms_norm(x, ln_gain)
    proj = h @ w_in
    offs = np.cumsum(IN_SPLITS)[:-1].tolist()
    z_pool, g_pool, q, g_nsa, kv, kvw, g_br = jnp.split(proj, offs, axis=-1)
    q = partial_rope(rms_norm(q.reshape(B, L, N_HEADS, HEAD_DIM), q_norm), pos)
    kv = kv.reshape(B, L, N_CACHED, N_KV_HEADS, HEAD_DIM)
    kvw = kvw.reshape(B, L, N_WIN, N_KV_HEADS, HEAD_DIM)
    k_cmp = partial_rope(rms_norm(kv[:, :, 0], k_norm[0]), pos)
    k_slc = partial_rope(rms_norm(kv[:, :, 2], k_norm[1]), pos)
    k_win = partial_rope(rms_norm(kvw[:, :, 0], k_norm[2]), pos)
    kv = jnp.stack([k_cmp, kv[:, :, 1], k_slc, kv[:, :, 3]], axis=2)
    kvw = jnp.stack([k_win, kvw[:, :, 1]], axis=2)
    gates = jax.nn.sigmoid(g_br.astype(jnp.float32)).reshape(B, L, N_KV_HEADS, GQA, N_BRANCH)
    return z_pool, g_pool, q, g_nsa, kv, kvw, gates


def pool_mixer(hist, z, pos, w_pool, pool_scale):
    B, L = z.shape[:2]
    ext = jnp.concatenate([hist, z], axis=1).astype(jnp.float32)
    cs = jnp.pad(jnp.cumsum(ext, axis=1), ((0, 0), (1, 0), (0, 0)))
    end = cs[:, POOL_HIST + 1:]
    means = []
    for gi, w in enumerate(POOL_WINDOWS):
        sl = slice(gi * POOL_GROUP_DIM, (gi + 1) * POOL_GROUP_DIM)
        start = cs[:, POOL_HIST + 1 - w: POOL_HIST + 1 - w + L, sl]
        cnt = jnp.minimum(w, pos + 1).astype(jnp.float32)[None, :, None]
        means.append((end[..., sl] - start) / cnt)
    pooled = (jnp.concatenate(means, axis=-1) - z.astype(jnp.float32)).reshape(B, L, POOL_GROUPS, POOL_GROUP_DIM)
    mixed = jnp.einsum('blgc,gce->blge', pooled, w_pool.astype(jnp.float32)).reshape(B, L, POOL_WIDTH)
    return (mixed * pool_scale.astype(jnp.float32)).astype(z.dtype)


def compress(k, w1, b1, w2, n_blocks):
    B, T, H, D = k.shape
    sub = k.reshape(B, T // CMP_STRIDE, CMP_STRIDE, H, D)
    first = jnp.einsum('bjshd,sde->bjhe', sub[:, :n_blocks], w1[:CMP_STRIDE])
    second = jnp.einsum('bjshd,sde->bjhe', sub[:, 1:n_blocks + 1], w1[CMP_STRIDE:])
    hid = jax.nn.silu(first + second + b1)
    return jnp.einsum('bnhe,ef->bnhf', hid, w2)


def compressed_kv(k, v, t_len, w1, b1, w2):
    n_blocks = (t_len - CMP_BLOCK) // CMP_STRIDE + 1
    kc = compress(k, w1[0], b1[0], w2[0], n_blocks)
    vc = compress(v, w1[1], b1[1], w2[1], n_blocks)
    cmp_end = jnp.arange(n_blocks) * CMP_STRIDE + CMP_BLOCK - 1
    return kc, vc, cmp_end


def nsa_attend(q, gates, qpos, kc, vc, cmp_end, k_slc, v_slc, k_win, v_win, win_pos):
    B, C = q.shape[:2]
    scale = HEAD_DIM ** -0.5
    qg = q.reshape(B, C, N_KV_HEADS, GQA, HEAD_DIM)
    s = jnp.einsum('bchgd,bnhd->bchgn', qg, kc) * scale
    mask_c = cmp_end[None, :] <= qpos[:, None]
    p_c = masked_softmax(s, mask_c[None, :, None, None, :])
    o_cmp = jnp.einsum('bchgn,bnhd->bchgd', p_c.astype(vc.dtype), vc)
    n_cmp = kc.shape[1]
    n_sel = k_slc.shape[1] // SEL_BLOCK
    cstart = jnp.arange(n_cmp) * CMP_STRIDE
    bstart = jnp.arange(n_sel) * SEL_BLOCK
    overlap = jnp.clip(jnp.minimum(cstart[:, None] + CMP_BLOCK, bstart[None, :] + SEL_BLOCK)
                       - jnp.maximum(cstart[:, None], bstart[None, :]), 0).astype(jnp.float32) / CMP_BLOCK
    imp = jnp.einsum('bchgn,nj->bchj', p_c, overlap)
    blk = jnp.arange(n_sel)[None, :]
    qblk = (qpos // SEL_BLOCK)[:, None]
    valid = bstart[None, :] <= qpos[:, None]
    forced = (blk == 0) | (blk == qblk) | (blk == qblk - 1)
    score = jnp.where(valid[None, :, None, :], jnp.where(forced[None, :, None, :], FORCE_SCORE, imp), -jnp.inf)
    n_top = min(SEL_TOP_N, n_sel)
    top_val, top_idx = lax.top_k(score, n_top)
    kb = k_slc.reshape(B, n_sel, SEL_BLOCK, N_KV_HEADS, HEAD_DIM)
    vb = v_slc.reshape(B, n_sel, SEL_BLOCK, N_KV_HEADS, HEAD_DIM)
    bi = jnp.arange(B)[:, None, None, None]
    hi = jnp.arange(N_KV_HEADS)[None, None, :, None]
    k_sel = kb[bi, top_idx, :, hi].reshape(B, C, N_KV_HEADS, n_top * SEL_BLOCK, HEAD_DIM)
    v_sel = vb[bi, top_idx, :, hi].reshape(B, C, N_KV_HEADS, n_top * SEL_BLOCK, HEAD_DIM)
    kpos = top_idx[..., None] * SEL_BLOCK + jnp.arange(SEL_BLOCK)
    mask_s = (jnp.isfinite(top_val)[..., None] & (kpos <= qpos[None, :, None, None, None]))
    mask_s = mask_s.reshape(B, C, N_KV_HEADS, n_top * SEL_BLOCK)
    s = jnp.einsum('bchgd,bchmd->bchgm', qg, k_sel) * scale
    p = masked_softmax(s, mask_s[:, :, :, None, :])
    o_slc = jnp.einsum('bchgm,bchmd->bchgd', p.astype(v_sel.dtype), v_sel)
    s = jnp.einsum('bchgd,bmhd->bchgm', qg, k_win) * scale
    rel = qpos[:, None] - win_pos[None, :]
    mask_w = (rel >= 0) & (rel < WINDOW) & (win_pos[None, :] >= 0)
    p = masked_softmax(s, mask_w[None, :, None, None, :])
    o_win = jnp.einsum('bchgm,bmhd->bchgd', p.astype(v_win.dtype), v_win)
    o = (gates[..., 0:1] * o_cmp.astype(jnp.float32) + gates[..., 1:2] * o_slc.astype(jnp.float32)
         + gates[..., 2:3] * o_win.astype(jnp.float32))
    return o.reshape(B, C, NSA_WIDTH).astype(q.dtype)


def prompt_nsa(q, gates, kc, vc, cmp_end, k_slc, v_slc, k_win, v_win):
    B, S = q.shape[:2]
    q_chunk = min(Q_CHUNK, S)
    pad = ((0, 0), (WINDOW, 0), (0, 0), (0, 0))
    kw_pad = jnp.pad(k_win, pad)
    vw_pad = jnp.pad(v_win, pad)

    def one_chunk(c0):
        qpos = c0 + jnp.arange(q_chunk)
        wpos = c0 - WINDOW + jnp.arange(WINDOW + q_chunk)
        return nsa_attend(lax.dynamic_slice_in_dim(q, c0, q_chunk, 1),
                          lax.dynamic_slice_in_dim(gates, c0, q_chunk, 1),
                          qpos, kc, vc, cmp_end, k_slc, v_slc,
                          lax.dynamic_slice_in_dim(kw_pad, c0, WINDOW + q_chunk, 1),
                          lax.dynamic_slice_in_dim(vw_pad, c0, WINDOW + q_chunk, 1), wpos)

    out = lax.map(one_chunk, jnp.arange(S // q_chunk) * q_chunk)
    return out.transpose(1, 0, 2, 3).reshape(B, S, NSA_WIDTH)


def mix_out(x, pool_out, g_pool, nsa_out, g_nsa, w_out):
    gated = jnp.concatenate([pool_out * jax.nn.silu(g_pool), nsa_out * jax.nn.silu(g_nsa)], axis=-1)
    return x + gated @ w_out


def setup_inputs(seed: int = 0) -> dict:
    key = jax.random.key(seed)
    ks = jax.random.split(key, 16)
    n_pages = PAST_LEN // PAGE_SIZE
    n_pool_pages = (DEC_BATCH * n_pages * 5) // 4
    win_buf = min(WINDOW, PAST_LEN)

    def nrm(k, shape, s=1.0):
        return jax.random.normal(k, shape, jnp.float32) * s

    page_table = jax.random.permutation(ks[3], n_pool_pages)[: DEC_BATCH * n_pages]
    page_table = page_table.reshape(DEC_BATCH, n_pages).astype(jnp.int32)
    return {
        'x_prompt': nrm(ks[0], (BATCH, SEQ, D_MODEL)),
        'x_sample': nrm(ks[1], (DEC_BATCH, DEC_SEQ, D_MODEL)),
        'cache_kv': nrm(ks[2], (DEPTH, n_pool_pages, PAGE_SIZE, N_CACHED, N_KV_HEADS, HEAD_DIM)),
        'state_kv_win': nrm(ks[4], (DEPTH, DEC_BATCH, win_buf, N_WIN, N_KV_HEADS, HEAD_DIM)),
        'state_pool': nrm(ks[5], (DEPTH, DEC_BATCH, POOL_HIST, POOL_WIDTH)),
        'page_table': page_table,
        'ln_gain': 1.0 + nrm(ks[6], (DEPTH, D_MODEL), 0.05),
        'w_in': nrm(ks[7], (DEPTH, D_MODEL, IN_WIDTH), D_MODEL ** -0.5),
        'q_norm': 1.0 + nrm(ks[8], (DEPTH, HEAD_DIM), 0.05),
        'k_norm': 1.0 + nrm(ks[9], (DEPTH, N_BRANCH, HEAD_DIM), 0.05),
        'w_cmp1': nrm(ks[10], (DEPTH, 2, CMP_BLOCK, HEAD_DIM, HEAD_DIM), (CMP_BLOCK * HEAD_DIM) ** -0.5),
        'b_cmp1': nrm(ks[11], (DEPTH, 2, HEAD_DIM), 0.02),
        'w_cmp2': nrm(ks[12], (DEPTH, 2, HEAD_DIM, HEAD_DIM), HEAD_DIM ** -0.5),
        'w_pool': nrm(ks[13], (DEPTH, POOL_GROUPS, POOL_GROUP_DIM, POOL_GROUP_DIM), POOL_GROUP_DIM ** -0.5),
        'pool_scale': 1.0 + nrm(ks[14], (DEPTH, POOL_WIDTH), 0.1),
        'w_out': nrm(ks[15], (DEPTH, MIX_WIDTH, D_MODEL), MIX_WIDTH ** -0.5),
    }


def reference(x_prompt, x_sample, cache_kv, state_kv_win, state_pool, page_table,
              ln_gain, w_in, q_norm, k_norm, w_cmp1, b_cmp1, w_cmp2, w_pool, pool_scale, w_out):
    B, S, _ = x_prompt.shape
    Bd, Ld, _ = x_sample.shape
    past = page_table.shape[1] * cache_kv.shape[2]
    win_buf = state_kv_win.shape[2]
    pos_p = jnp.arange(S)
    pos_s = past + jnp.arange(Ld)
    sel_pad = (-Ld) % SEL_BLOCK
    hp, hs = x_prompt, x_sample
    kv_p_l, kv_s_l, win_p_l, win_s_l, pool_p_l, pool_s_l = [], [], [], [], [], []
    for layer in range(DEPTH):
        zp, gpp, qp, gnp, kvp, kvwp, gbp = project_tokens(hp, pos_p, ln_gain[layer], w_in[layer],
                                                           q_norm[layer], k_norm[layer])
        pool_p = pool_mixer(jnp.zeros((B, POOL_HIST, POOL_WIDTH), zp.dtype), zp, pos_p,
                            w_pool[layer], pool_scale[layer])
        kc_p, vc_p, cend_p = compressed_kv(kvp[:, :, 0], kvp[:, :, 1], S,
                                           w_cmp1[layer], b_cmp1[layer], w_cmp2[layer])
        nsa_p = prompt_nsa(qp, gbp, kc_p, vc_p, cend_p, kvp[:, :, 2], kvp[:, :, 3],
                           kvwp[:, :, 0], kvwp[:, :, 1])
        zs, gps, qs, gns, kvs, kvws, gbs = project_tokens(hs, pos_s, ln_gain[layer], w_in[layer],
                                                           q_norm[layer], k_norm[layer])
        pool_s = pool_mixer(state_pool[layer], zs, pos_s, w_pool[layer], pool_scale[layer])
        past_kv = cache_kv[layer, page_table].reshape(Bd, past, N_CACHED, N_KV_HEADS, HEAD_DIM)
        kv_all = jnp.concatenate([past_kv, jnp.pad(kvs, ((0, 0), (0, sel_pad), (0, 0), (0, 0), (0, 0)))], axis=1)
        kc_s, vc_s, cend_s = compressed_kv(kv_all[:, :, 0], kv_all[:, :, 1], past + Ld,
                                           w_cmp1[layer], b_cmp1[layer], w_cmp2[layer])
        win_all = jnp.concatenate([state_kv_win[layer], kvws], axis=1)
        wpos = past - win_buf + jnp.arange(win_buf + Ld)
        nsa_s = nsa_attend(qs, gbs, pos_s, kc_s, vc_s, cend_s, kv_all[:, :, 2], kv_all[:, :, 3],
                           win_all[:, :, 0], win_all[:, :, 1], wpos)
        hp = mix_out(hp, pool_p, gpp, nsa_p, gnp, w_out[layer])
        hs = mix_out(hs, pool_s, gps, nsa_s, gns, w_out[layer])
        kv_p_l.append(kvp)
        kv_s_l.append(kvs)
        win_p_l.append(kvwp[:, -min(WINDOW, S):])
        win_s_l.append(win_all[:, -win_buf:])
        pool_p_l.append(zp[:, -POOL_HIST:])
        pool_s_l.append(jnp.concatenate([state_pool[layer], zs], axis=1)[:, -POOL_HIST:])
    kv_prompt = jnp.stack(kv_p_l, 0)
    kv_sample = jnp.stack(kv_s_l, 0)
    win_prompt = jnp.stack(win_p_l, 0)
    win_sample = jnp.stack(win_s_l, 0)
    pool_prompt = jnp.stack(pool_p_l, 0)
    pool_sample = jnp.stack(pool_s_l, 0)
    return (hp, hs, kv_prompt, kv_sample, win_prompt, win_sample, pool_prompt, pool_sample)
```

```python
import functools

import numpy as np
import jax
import jax.numpy as jnp
from jax import lax
from jax.experimental import pallas as pl
from jax.experimental.pallas import tpu as pltpu

F32 = jnp.float32
BF16 = jnp.bfloat16

D_MODEL = 1024
POOL_WIDTH = 256
POOL_WINDOWS = (2, 4, 8, 16)
POOL_GROUP_DIM = 64
POOL_HIST = 15
HEAD_DIM = 64
N_HEADS = 12
N_KV_HEADS = 4
GQA = 3
NSA_WIDTH = 768
KV_WIDTH = 256
CMP_BLOCK = 32
CMP_STRIDE = 16
SEL_BLOCK = 64
SEL_TOP_N = 16
WINDOW = 512
ROPE_DIM = 16
ROPE_THETA = 500000.0
EPS = 1e-6
FORCE_SCORE = 1e4
IN_SPLITS = (256, 256, 768, 768, 1024, 512, 36)
IN_OFFS = tuple(int(v) for v in np.cumsum((0,) + IN_SPLITS))
SCALE = HEAD_DIM ** -0.5

LANES = 128
VMEM_LIMIT_BYTES = 56 * 1024 * 1024

NEG_INF = float("-inf")
N_SLABS = 2 * KV_WIDTH // LANES


def _cparams(semantics):
    return pltpu.CompilerParams(dimension_semantics=semantics, vmem_limit_bytes=VMEM_LIMIT_BYTES)


def _dot(a, b):
    return jnp.dot(a, b, preferred_element_type=F32)


def _dot_nt(a, b):
    return lax.dot_general(a, b, (((1,), (1,)), ((), ())), preferred_element_type=F32)


def _split_dot(x, w_bf16):
    hi = x.astype(BF16)
    lo = (x - hi.astype(F32)).astype(BF16)
    return _dot(hi, w_bf16) + _dot(lo, w_bf16)


def _silu(x):
    return x * jax.nn.sigmoid(x)


def _safe_softmax(s):
    m = jnp.max(s, axis=-1, keepdims=True)
    m = jnp.where(m == NEG_INF, 0.0, m)
    e = jnp.exp(s - m)
    return e / jnp.maximum(jnp.sum(e, axis=-1, keepdims=True), 1e-30)


def _head_norm_rope(v, gain, bd, cos, sin_lo, sin_hi):
    outs = []
    for c in range(v.shape[1] // LANES):
        vc = v[:, c * LANES:(c + 1) * LANES]
        ms = _split_dot(vc * vc, bd)
        n = vc * lax.rsqrt(ms + EPS) * gain[:, c * LANES:(c + 1) * LANES]
        outs.append(n * cos + pltpu.roll(n, LANES - 8, 1) * sin_lo + pltpu.roll(n, 8, 1) * sin_hi)
    return jnp.concatenate(outs, axis=1)


def _head_norm_rope_t(vt, gain_t, cos_t, sin_t):
    outs = []
    for c in range(vt.shape[1] // LANES):
        sl = slice(c * LANES, (c + 1) * LANES)
        v = vt[:, sl].reshape(N_KV_HEADS, HEAD_DIM, LANES)
        ms = jnp.mean(v * v, axis=1, keepdims=True)
        n = v * lax.rsqrt(ms + EPS) * gain_t.reshape(N_KV_HEADS, HEAD_DIM, LANES)
        half = ROPE_DIM // 2
        x1, x2 = n[:, 0:half, :], n[:, half:ROPE_DIM, :]
        cos, sin = cos_t[:, sl][None], sin_t[:, sl][None]
        r = jnp.concatenate([x1 * cos - x2 * sin, x2 * cos + x1 * sin, n[:, ROPE_DIM:, :]], axis=1)
        outs.append(r.reshape(N_KV_HEADS * HEAD_DIM, LANES))
    return jnp.concatenate(outs, axis=1)


def _project_kernel(x_ref, gain_ref, w_ref, wt_ref, qg_ref, kgt_ref, bd_ref, cos_ref, slo_ref, shi_ref, cost_ref, sint_ref,
                    z_ref, sgp_ref, q_ref, sgn_ref, gate_ref, kvt_ref, kvwt_ref, *row_refs):
    x = x_ref[0]
    ms = jnp.mean(x * x, axis=-1, keepdims=True)
    hb = (x * lax.rsqrt(ms + EPS) * gain_ref[...]).astype(BF16)

    def proj(lo, hi):
        return _dot(hb, w_ref[:, lo:hi])

    o = IN_OFFS
    z_ref[0] = proj(o[0], o[1])
    sgp_ref[0] = _silu(proj(o[1], o[2]))
    q_ref[0] = _head_norm_rope(proj(o[2], o[3]), qg_ref[...], bd_ref[...], cos_ref[...], slo_ref[...], shi_ref[...])
    sgn_ref[0] = _silu(proj(o[3], o[4]))
    gate_ref[0] = jax.nn.sigmoid(proj(o[6], o[7]))
    key_gain = {0: 0, 2: 1, 4: 2}
    cos_t, sin_t = cost_ref[...], sint_ref[...]
    for grp in range(6):
        val_t = _dot_nt(wt_ref[grp * KV_WIDTH:(grp + 1) * KV_WIDTH, :], hb)
        if grp in key_gain:
            val_t = _head_norm_rope_t(val_t, kgt_ref[key_gain[grp]], cos_t, sin_t)
        dst, r0 = (kvt_ref, grp * KV_WIDTH) if grp < 4 else (kvwt_ref, (grp - 4) * KV_WIDTH)
        dst[0, r0:r0 + KV_WIDTH, :] = val_t
        if row_refs:
            row_refs[0 if grp < 4 else 1][0, :, r0:r0 + KV_WIDTH] = val_t.T


def _rope_tables(pos):
    half = ROPE_DIM // 2
    freqs = ROPE_THETA ** (-jnp.arange(half, dtype=F32) / half)
    ang = pos.astype(F32)[:, None] * freqs[None, :]
    cos8, sin8 = jnp.cos(ang), jnp.sin(ang)
    p = pos.shape[0]
    rest = HEAD_DIM - ROPE_DIM
    cos = jnp.concatenate([cos8, cos8, jnp.ones((p, rest), F32)], axis=1)
    slo = jnp.concatenate([-sin8, jnp.zeros((p, HEAD_DIM - half), F32)], axis=1)
    shi = jnp.concatenate([jnp.zeros((p, half), F32), sin8, jnp.zeros((p, rest), F32)], axis=1)
    return tuple(jnp.tile(t, (1, LANES // HEAD_DIM)) for t in (cos, slo, shi)) + (cos8.T, sin8.T)


def _project(x, pos, tm, ln_gain, w_bf16, q_norm, k_norm, emit_rows):
    b, s, _ = x.shape
    assert pos.shape[0] == s and tm % LANES == 0
    cos, slo, shi, cos_t, sin_t = _rope_tables(pos)
    tab_spec = pl.BlockSpec((tm, LANES), lambda bb, i: (i, 0))
    tab_t_spec = pl.BlockSpec((ROPE_DIM // 2, tm), lambda bb, i: (0, i))
    bd = jnp.asarray(np.kron(np.eye(LANES // HEAD_DIM), np.full((HEAD_DIM, HEAD_DIM), 1.0 / HEAD_DIM)), BF16)
    qg = jnp.tile(q_norm, N_HEADS)[None, :]
    kgt = jnp.broadcast_to(jnp.tile(k_norm, (1, N_KV_HEADS))[:, :, None], (k_norm.shape[0], KV_WIDTH, LANES))
    gain = ln_gain[None, :]
    wt = w_bf16[:, IN_OFFS[4]:IN_OFFS[6]].T

    def full(a):
        return pl.BlockSpec(a.shape, lambda bb, i: (0,) * a.ndim)

    def rows(w):
        return pl.BlockSpec((1, tm, w), lambda bb, i: (bb, i, 0))

    def cols(f):
        return pl.BlockSpec((1, f, tm), lambda bb, i: (bb, 0, i))

    row_w = (256, 256, 768, 768, 36)
    out_specs = [rows(w) for w in row_w] + [cols(4 * KV_WIDTH), cols(2 * KV_WIDTH)]
    out_shape = ([jax.ShapeDtypeStruct((b, s, w), F32) for w in row_w]
                 + [jax.ShapeDtypeStruct((b, 4 * KV_WIDTH, s), F32), jax.ShapeDtypeStruct((b, 2 * KV_WIDTH, s), F32)])
    if emit_rows:
        out_specs += [rows(4 * KV_WIDTH), rows(2 * KV_WIDTH)]
        out_shape += [jax.ShapeDtypeStruct((b, s, 4 * KV_WIDTH), F32), jax.ShapeDtypeStruct((b, s, 2 * KV_WIDTH), F32)]
    return pl.pallas_call(
        _project_kernel,
        grid=(b, s // tm),
        in_specs=[rows(D_MODEL), full(gain), full(w_bf16), full(wt), full(qg), full(kgt), full(bd),
                  tab_spec, tab_spec, tab_spec, tab_t_spec, tab_t_spec],
        out_specs=out_specs,
        out_shape=out_shape,
        compiler_params=_cparams(("parallel", "parallel")),
        name="project",
    )(x, gain, w_bf16, wt, qg, kgt, bd, cos, slo, shi, cos_t, sin_t)


def _pool_prompt_kernel(z_ref, sgp_ref, wp_ref, scale_ref, out_ref, pad_ref):
    length = z_ref.shape[1]
    pad_ref[0:16, :] = jnp.zeros((16, POOL_WIDTH), F32)
    pad_ref[16:16 + length, :] = z_ref[0]
    z = pad_ref[16:16 + length, :]
    acc = z
    sums = {}
    for k in range(1, POOL_HIST + 1):
        acc = acc + pad_ref[16 - k:16 - k + length, :]
        if k + 1 in POOL_WINDOWS:
            sums[k + 1] = acc
    pos = lax.broadcasted_iota(jnp.int32, (length, 1), 0)
    lane = lax.broadcasted_iota(jnp.int32, (1, POOL_WIDTH), 1)
    means = None
    for gi, w in enumerate(POOL_WINDOWS):
        m = sums[w] / jnp.minimum(w, pos + 1).astype(F32)
        means = m if means is None else jnp.where(lane >= gi * POOL_GROUP_DIM, m, means)
    mixed = _dot((means - z).astype(BF16), wp_ref[...])
    out_ref[0] = mixed * scale_ref[...] * sgp_ref[0]


def _pool_prompt(z, sgp, wp_bd, scale):
    b, length, _ = z.shape
    blk = pl.BlockSpec((1, length, POOL_WIDTH), lambda i: (i, 0, 0))
    return pl.pallas_call(
        _pool_prompt_kernel,
        grid=(b,),
        in_specs=[blk, blk, pl.BlockSpec(wp_bd.shape, lambda i: (0, 0)), pl.BlockSpec(scale.shape, lambda i: (0, 0))],
        out_specs=blk,
        out_shape=jax.ShapeDtypeStruct(z.shape, F32),
        scratch_shapes=[pltpu.VMEM((16 + length, POOL_WIDTH), F32)],
        compiler_params=_cparams(("parallel",)),
        name="pool_prompt",
    )(z, sgp, wp_bd, scale)


def _compress_accumulate(load_rows, w1_ref, slot, nj):
    acc = jnp.zeros((nj, 2 * LANES), F32)
    for s2 in range(CMP_STRIDE // 2):
        lhs = jnp.concatenate([load_rows(2 * s2), load_rows(2 * s2 + 1)], axis=1).astype(BF16)
        acc = acc + _dot(lhs, w1_ref[slot, s2])
    return acc


def _compress_finish(acc, b1, w2):
    nj = acc.shape[0]
    second = pltpu.roll(acc[:, LANES:], nj - 1, 0)
    hid = _silu(acc[:, :LANES] + second + b1)
    return _dot(hid.astype(BF16), w2)


def _compress_prompt_kernel(kvt_ref, w1_ref, b1_ref, w2_ref, kc_ref, vc_ref, xbuf):
    length = kvt_ref.shape[2]
    nj = length // CMP_STRIDE
    for sl in range(N_SLABS):
        for tt in range(length // LANES):
            xbuf[sl, tt * LANES:(tt + 1) * LANES, :] = kvt_ref[0, sl * LANES:(sl + 1) * LANES, tt * LANES:(tt + 1) * LANES].T
    for sl in range(N_SLABS):
        slot, out_ref = sl // 2, (kc_ref, vc_ref)[sl // 2]
        acc = _compress_accumulate(lambda s, sl=sl: xbuf[sl, pl.ds(s, nj, stride=CMP_STRIDE), :], w1_ref, slot, nj)
        out_ref[0, :, (sl % 2) * LANES:(sl % 2 + 1) * LANES] = _compress_finish(acc, b1_ref[slot], w2_ref[slot])


def _compress_prompt(kvt, w1cat, b1t, w2bd):
    b, _, length = kvt.shape
    nj = length // CMP_STRIDE

    def full(a):
        return pl.BlockSpec(a.shape, lambda bb: (0,) * a.ndim)

    out_blk = pl.BlockSpec((1, nj, KV_WIDTH), lambda bb: (bb, 0, 0))
    return pl.pallas_call(
        _compress_prompt_kernel,
        grid=(b,),
        in_specs=[pl.BlockSpec((1, 2 * KV_WIDTH, length), lambda bb: (bb, 0, 0)), full(w1cat), full(b1t), full(w2bd)],
        out_specs=[out_blk, out_blk],
        out_shape=[jax.ShapeDtypeStruct((b, nj, KV_WIDTH), F32)] * 2,
        scratch_shapes=[pltpu.VMEM((N_SLABS, length, LANES), F32)],
        compiler_params=_cparams(("parallel",)),
        name="compress_prompt",
    )(kvt, w1cat, b1t, w2bd)


def _compress_weights(w_cmp1, b_cmp1, w_cmp2):
    eye2 = jnp.eye(LANES // HEAD_DIM, dtype=F32)

    def bd2(m):
        return jnp.kron(eye2, m)

    w1cat = []
    for slot in range(2):
        per = []
        for s2 in range(CMP_STRIDE // 2):
            blocks = []
            for s in (2 * s2, 2 * s2 + 1):
                blocks.append(jnp.concatenate([bd2(w_cmp1[slot, s]), bd2(w_cmp1[slot, CMP_STRIDE + s])], axis=1))
            per.append(jnp.concatenate(blocks, axis=0))
        w1cat.append(jnp.stack(per))
    w1cat = jnp.stack(w1cat).astype(BF16)
    b1t = jnp.tile(b_cmp1, (1, LANES // HEAD_DIM))[:, None, :]
    w2bd = jnp.stack([bd2(w_cmp2[0]), bd2(w_cmp2[1])]).astype(BF16)
    return w1cat, b1t, w2bd


def _overlap_matrix(n_cmp_pad, n_cmp, n_sel, n_sel_pad):
    cstart = np.arange(n_cmp_pad)[:, None] * CMP_STRIDE
    bstart = np.arange(n_sel_pad)[None, :] * SEL_BLOCK
    ov = np.clip(np.minimum(cstart + CMP_BLOCK, bstart + SEL_BLOCK) - np.maximum(cstart, bstart), 0, None) / CMP_BLOCK
    ov = ov * (np.arange(n_cmp_pad)[:, None] < n_cmp) * (np.arange(n_sel_pad)[None, :] < n_sel)
    return jnp.asarray(ov, BF16)


def _topk_mask(score, n_iter):
    lane = lax.broadcasted_iota(jnp.int32, score.shape, 1)
    rank = jnp.zeros(score.shape, F32)
    for j in range(n_iter):
        col = score[:, j:j + 1]
        ahead = (col > score) | ((col == score) & (lane > j))
        rank = rank + jnp.where(ahead, 1.0, 0.0)
    return rank, (rank < SEL_TOP_N) & (score > NEG_INF)


TQ = 128
WIN_TILES = WINDOW // TQ + 1


def _nsa_prompt_kernel(q_ref, gate_ref, kc_ref, vc_ref, kst_ref, vst_ref, *rest):
    win_refs, (ov_ref, ex_ref, out_ref) = rest[:WIN_TILES], rest[WIN_TILES:]
    i = pl.program_id(1)
    seq = kst_ref.shape[2]
    n_cmp = kc_ref.shape[1]
    qpos = i * TQ + lax.broadcasted_iota(jnp.int32, (TQ, 1), 0)
    lane = lax.broadcasted_iota(jnp.int32, (1, LANES), 1)
    lane_half = lane // HEAD_DIM

    ncol = lax.broadcasted_iota(jnp.int32, (1, n_cmp), 1)
    bias_c = jnp.where(ncol * CMP_STRIDE + CMP_BLOCK - 1 <= qpos, 0.0, NEG_INF)
    causal = lax.broadcasted_iota(jnp.int32, (1, seq), 1) <= qpos
    bias_w = []
    for u in range(WIN_TILES):
        kb = i - (WIN_TILES - 1) + u
        rel = qpos - (kb * TQ + lane)
        bias_w.append(jnp.where((kb >= 0) & (rel >= 0) & (rel < WINDOW), 0.0, NEG_INF))
    bias_w = jnp.concatenate(bias_w, axis=1)
    qblk = qpos // SEL_BLOCK
    valid_j = lane * SEL_BLOCK <= qpos
    forced = (lane == 0) | (lane == qblk) | (lane == qblk - 1)
    gates = gate_ref[0]

    def softmax3(s, bias):
        s = (s.reshape(GQA, TQ, s.shape[-1]) + bias[None]).reshape(GQA * TQ, s.shape[-1])
        return _safe_softmax(s)

    out_chunks = [jnp.zeros((TQ, LANES), F32) for _ in range(NSA_WIDTH // LANES)]
    for h in range(N_KV_HEADS):
        pair, half = h // 2, h % 2
        lo, hi = pair * LANES, (pair + 1) * LANES
        rows = []
        for g in range(GQA):
            qh = h * GQA + g
            x = q_ref[0, :, (qh // 2) * LANES:(qh // 2 + 1) * LANES] * SCALE
            if qh % 2 != half:
                x = pltpu.roll(x, HEAD_DIM, 1)
            rows.append(jnp.where(lane_half == half, x, 0.0))
        qz = jnp.concatenate(rows, axis=0).astype(BF16)

        p_c = softmax3(_dot_nt(qz, kc_ref[0, :, lo:hi].astype(BF16)), bias_c)
        o_cmp = _dot(p_c.astype(BF16), vc_ref[0, :, lo:hi].astype(BF16))
        p_sum = p_c[0:TQ] + p_c[TQ:2 * TQ] + p_c[2 * TQ:3 * TQ]
        imp = _split_dot(p_sum, ov_ref[...])
        score = jnp.where(valid_j, jnp.where(forced, FORCE_SCORE, imp), NEG_INF)
        _, sel = _topk_mask(score, seq // SEL_BLOCK)
        selw = _dot(jnp.where(sel, 1.0, 0.0).astype(BF16), ex_ref[...])
        bias_s = jnp.where((selw > 0.5) & causal, 0.0, NEG_INF)

        p_s = softmax3(_dot(qz, kst_ref[0, lo:hi, :].astype(BF16)), bias_s)
        o_slc = _dot_nt(p_s.astype(BF16), vst_ref[0, lo:hi, :].astype(BF16))
        kwt = jnp.concatenate([w[0, lo:hi, :] for w in win_refs], axis=1).astype(BF16)
        vwt = jnp.concatenate([w[0, KV_WIDTH + lo:KV_WIDTH + hi, :] for w in win_refs], axis=1).astype(BF16)
        p_w = softmax3(_dot(qz, kwt), bias_w)
        o_win = _dot_nt(p_w.astype(BF16), vwt)

        for g in range(GQA):
            qh = h * GQA + g
            gi = qh * 3
            sl = slice(g * TQ, (g + 1) * TQ)
            o = (gates[:, gi:gi + 1] * o_cmp[sl] + gates[:, gi + 1:gi + 2] * o_slc[sl] + gates[:, gi + 2:gi + 3] * o_win[sl])
            if qh % 2 != half:
                o = pltpu.roll(o, HEAD_DIM, 1)
            out_chunks[qh // 2] = out_chunks[qh // 2] + jnp.where(lane_half == qh % 2, o, 0.0)
    for c, val in enumerate(out_chunks):
        out_ref[0, :, c * LANES:(c + 1) * LANES] = val


def _nsa_prompt(q, gates, kc, vc, kvt, kvwt):
    b, seq, _ = q.shape
    n_cmp = kc.shape[1]
    n_sel = seq // SEL_BLOCK
    assert n_sel <= LANES and seq % TQ == 0
    ov = _overlap_matrix(n_cmp, (seq - CMP_BLOCK) // CMP_STRIDE + 1, n_sel, LANES)
    ex = jnp.asarray((np.arange(LANES)[:, None] == (np.arange(seq)[None, :] // SEL_BLOCK)), BF16)

    def tile(w):
        return pl.BlockSpec((1, TQ, w), lambda bb, i: (bb, i, 0))

    def whole(rows):
        return pl.BlockSpec((1, rows, KV_WIDTH), lambda bb, i: (bb, 0, 0))

    def feat_rows(row_blk):
        return pl.BlockSpec((1, KV_WIDTH, seq), lambda bb, i, rb=row_blk: (bb, rb, 0))

    def win_tile(u):
        return pl.BlockSpec((1, 2 * KV_WIDTH, TQ), lambda bb, i, u=u: (bb, 0, jnp.maximum(i - (WIN_TILES - 1) + u, 0)))

    def full(a):
        return pl.BlockSpec(a.shape, lambda bb, i: (0,) * a.ndim)

    return pl.pallas_call(
        _nsa_prompt_kernel,
        grid=(b, seq // TQ),
        in_specs=([tile(NSA_WIDTH), tile(gates.shape[-1]), whole(n_cmp), whole(n_cmp), feat_rows(2), feat_rows(3)]
                  + [win_tile(u) for u in range(WIN_TILES)] + [full(ov), full(ex)]),
        out_specs=tile(NSA_WIDTH),
        out_shape=jax.ShapeDtypeStruct((b, seq, NSA_WIDTH), F32),
        compiler_params=_cparams(("parallel", "arbitrary")),
        name="nsa_prompt",
    )(q, gates, kc, vc, kvt, kvt, *([kvwt] * WIN_TILES), ov, ex)


def _out_proj_kernel(x_ref, gp_ref, o_ref, sgn_ref, w_ref, y_ref):
    gn = (o_ref[...] * sgn_ref[...]).astype(BF16)
    y_ref[...] = (x_ref[...] + _dot(gp_ref[...].astype(BF16), w_ref[0:POOL_WIDTH, :])
                  + _dot(gn, w_ref[POOL_WIDTH:, :]))


def _out_proj(x, gp, o_nsa, sgn, w_bf16, tm):
    t = x.shape[0]

    def rows(w):
        return pl.BlockSpec((tm, w), lambda i: (i, 0))

    return pl.pallas_call(
        _out_proj_kernel,
        grid=(t // tm,),
        in_specs=[rows(D_MODEL), rows(POOL_WIDTH), rows(NSA_WIDTH), rows(NSA_WIDTH),
                  pl.BlockSpec(w_bf16.shape, lambda i: (0, 0))],
        out_specs=rows(D_MODEL),
        out_shape=jax.ShapeDtypeStruct((t, D_MODEL), F32),
        compiler_params=_cparams(("parallel",)),
        name="out_proj",
    )(x, gp, o_nsa, sgn, w_bf16)


ROWS = GQA * 8
CH_PAGES = 32


def _page_copy(cache_ref, xt, sem, page, p, slot):
    return pltpu.make_async_copy(cache_ref.at[page, pl.ds(0, 2 * KV_WIDTH), :], xt.at[slot, p], sem.at[slot])


def _compress_decode_kernel(pt_ref, cache_ref, qbd_ref, w1_ref, b1_ref, w2_ref, ov_ref, ocmp_ref, idx_ref,
                            xt, xbuf, fg, sem, *, n_pages, page_size):
    b, c = pl.program_id(0), pl.program_id(1)
    nb, nch = pl.num_programs(0), pl.num_programs(1)
    t = b * nch + c
    jch = CH_PAGES * page_size // CMP_STRIDE
    nj = nch * jch
    past = n_pages * page_size

    def start_chunk(step, slot):
        base = (step // nch) * n_pages + (step % nch) * CH_PAGES

        def body(p, carry):
            _page_copy(cache_ref, xt, sem, pt_ref[base + p], p, slot).start()
            return carry
        lax.fori_loop(0, CH_PAGES, body, 0)

    @pl.when(t == 0)
    def _():
        start_chunk(0, 0)

    @pl.when(t + 1 < nb * nch)
    def _():
        start_chunk(t + 1, (t + 1) % 2)

    slot = t % 2

    def wait_body(p, carry):
        _page_copy(cache_ref, xt, sem, 0, p, slot).wait()
        return carry
    lax.fori_loop(0, CH_PAGES, wait_body, 0)

    def xpose_body(p, carry):
        r0 = pl.multiple_of(p * page_size, page_size)
        for sl in range(N_SLABS):
            for tt in range(page_size // LANES):
                xbuf[sl, pl.ds(r0 + tt * LANES, LANES), :] = xt[slot, p, sl * LANES:(sl + 1) * LANES, tt * LANES:(tt + 1) * LANES].T
        return carry
    lax.fori_loop(0, CH_PAGES, xpose_body, 0)

    row0 = pl.multiple_of(c * jch, jch)
    for sl in range(N_SLABS):
        acc = _compress_accumulate(lambda s, sl=sl: xbuf[sl, pl.ds(s, jch, stride=CMP_STRIDE), :], w1_ref, sl // 2, jch)
        fg[sl, pl.ds(row0, jch), :] = acc

    @pl.when(c == nch - 1)
    def _():
        kcv = [_compress_finish(fg[sl], b1_ref[sl // 2], w2_ref[sl // 2]) for sl in range(N_SLABS)]
        kc = jnp.concatenate(kcv[0:2], axis=1).astype(BF16)
        vc = jnp.concatenate(kcv[2:4], axis=1).astype(BF16)
        s = _dot_nt(qbd_ref[0].astype(BF16), kc)
        ncol = lax.broadcasted_iota(jnp.int32, (1, nj), 1)
        s = s + jnp.where(ncol * CMP_STRIDE + CMP_BLOCK - 1 <= past, 0.0, NEG_INF)
        p = _safe_softmax(s)
        ocmp_ref[0] = _dot(p.astype(BF16), vc)
        p_sum = p[0:8] + p[8:16] + p[16:24]
        imp = _split_dot(p_sum, ov_ref[...])
        lane = lax.broadcasted_iota(jnp.int32, (1, imp.shape[1]), 1)
        qblk = past // SEL_BLOCK
        forced = (lane == 0) | (lane == qblk) | (lane == qblk - 1)
        score = jnp.where(lane * SEL_BLOCK <= past, jnp.where(forced, FORCE_SCORE, imp), NEG_INF)
        rank, sel = _topk_mask(score, qblk + 1)
        lanef = lane.astype(F32)
        lane16 = lax.broadcasted_iota(jnp.int32, (1, SEL_TOP_N), 1)
        idx = jnp.zeros((8, SEL_TOP_N), F32)
        for r in range(SEL_TOP_N):
            v = jnp.sum(jnp.where((rank == r) & sel, lanef, 0.0), axis=-1, keepdims=True)
            idx = jnp.where(lane16 == r, v, idx)
        idx_ref[0] = idx.astype(jnp.int32)


def _compress_decode(page_table, cache_t, qbd, w1cat, b1t, w2bd):
    nb, n_pages = page_table.shape
    page_size = cache_t.shape[2]
    assert n_pages % CH_PAGES == 0 and page_size % LANES == 0
    nch = n_pages // CH_PAGES
    past = n_pages * page_size
    nj = past // CMP_STRIDE
    n_sel = past // SEL_BLOCK + 1
    assert n_sel > SEL_TOP_N
    n_sel_pad = -(-n_sel // LANES) * LANES
    ov = _overlap_matrix(nj, (past + 1 - CMP_BLOCK) // CMP_STRIDE + 1, n_sel, n_sel_pad)

    def full(a):
        return pl.BlockSpec(a.shape, lambda bb, cc, pt: (0,) * a.ndim)

    def per_b(shape):
        return pl.BlockSpec((1,) + shape, lambda bb, cc, pt: (bb, 0, 0))

    grid_spec = pltpu.PrefetchScalarGridSpec(
        num_scalar_prefetch=1,
        grid=(nb, nch),
        in_specs=[pl.BlockSpec(memory_space=pl.ANY), per_b((ROWS, KV_WIDTH)), full(w1cat), full(b1t), full(w2bd), full(ov)],
        out_specs=[per_b((ROWS, KV_WIDTH)), per_b((8, SEL_TOP_N))],
        scratch_shapes=[pltpu.VMEM((2, CH_PAGES, 2 * KV_WIDTH, page_size), F32),
                        pltpu.VMEM((N_SLABS, CH_PAGES * page_size, LANES), F32),
                        pltpu.VMEM((N_SLABS, nj, 2 * LANES), F32),
                        pltpu.SemaphoreType.DMA((2,))],
    )
    return pl.pallas_call(
        functools.partial(_compress_decode_kernel, n_pages=n_pages, page_size=page_size),
        grid_spec=grid_spec,
        out_shape=[jax.ShapeDtypeStruct((nb, ROWS, KV_WIDTH), F32), jax.ShapeDtypeStruct((nb, 8, SEL_TOP_N), jnp.int32)],
        compiler_params=_cparams(("arbitrary", "arbitrary")),
        name="compress_decode",
    )(page_table.reshape(-1), cache_t, qbd, w1cat, b1t, w2bd, ov)


def _block_copies(cache_ref, kbuf, vbuf, sem, pt_ref, idx_ref, bb, h, r, slot, n_pages, page_size):
    bpp = page_size // SEL_BLOCK
    j = idx_ref[(bb * N_KV_HEADS + h) * SEL_TOP_N + r]
    jj = jnp.minimum(j, n_pages * bpp - 1)
    page = pt_ref[bb * n_pages + jj // bpp]
    dst0 = pl.multiple_of(r * page_size, page_size)
    row_k = (2 * N_KV_HEADS + h) * HEAD_DIM
    row_v = (3 * N_KV_HEADS + h) * HEAD_DIM
    return [pltpu.make_async_copy(cache_ref.at[page, pl.ds(row_k, HEAD_DIM), :],
                                  kbuf.at[slot, h, :, pl.ds(dst0, page_size)], sem.at[slot]),
            pltpu.make_async_copy(cache_ref.at[page, pl.ds(row_v, HEAD_DIM), :],
                                  vbuf.at[slot, h, :, pl.ds(dst0, page_size)], sem.at[slot])]


def _heads_to_lanes64(x, rowh):
    out = jnp.zeros((x.shape[0], HEAD_DIM), F32)
    for h in range(N_KV_HEADS):
        out = jnp.where(rowh == h, x[:, h * HEAD_DIM:(h + 1) * HEAD_DIM], out)
    return out


def _nsa_decode_kernel(pt_ref, idx_ref, cache_ref, qbd_ref, q4_ref, kvn_ref, kvwn_ref, wint_ref, gate_ref, ocmp_ref,
                       out_ref, kbuf, vbuf, sem, *, n_pages, page_size):
    b = pl.program_id(0)
    nb = pl.num_programs(0)
    bpp = page_size // SEL_BLOCK
    n_past_blocks = n_pages * bpp
    nk = SEL_TOP_N * page_size
    win_buf = wint_ref.shape[2]

    def start_gather(bb, slot):
        for h in range(N_KV_HEADS):
            def body(r, carry, h=h):
                for cp in _block_copies(cache_ref, kbuf, vbuf, sem, pt_ref, idx_ref, bb, h, r, slot, n_pages, page_size):
                    cp.start()
                return carry
            lax.fori_loop(0, SEL_TOP_N, body, 0)

    @pl.when(b == 0)
    def _():
        start_gather(0, 0)

    @pl.when(b + 1 < nb)
    def _():
        start_gather(b + 1, (b + 1) % 2)

    slot = b % 2
    for h in range(N_KV_HEADS):
        def wait_body(r, carry, h=h):
            for cp in _block_copies(cache_ref, kbuf, vbuf, sem, pt_ref, idx_ref, b, h, r, slot, n_pages, page_size):
                cp.wait()
            return carry
        lax.fori_loop(0, SEL_TOP_N, wait_body, 0)

    qbd = qbd_ref[0]
    rowh = lax.broadcasted_iota(jnp.int32, (ROWS, 1), 0) % 8
    kvn = kvn_ref[0]
    k_new, v_new = kvn[:, 2 * KV_WIDTH:3 * KV_WIDTH], kvn[:, 3 * KV_WIDTH:]
    col = lax.broadcasted_iota(jnp.int32, (1, nk), 1)
    col_r, col_blk = col // page_size, (col % page_size) // SEL_BLOCK

    s_new = jnp.sum(qbd * k_new, axis=-1, keepdims=True)
    o_slc = jnp.zeros((ROWS, HEAD_DIM), F32)
    for h in range(N_KV_HEADS):
        bias = jnp.full((1, nk), NEG_INF, F32)
        new_sel = jnp.zeros((1, 1), jnp.int32)
        for r in range(SEL_TOP_N):
            j = idx_ref[(b * N_KV_HEADS + h) * SEL_TOP_N + r]
            in_cache = j < n_past_blocks
            hit = (col_r == r) & (col_blk == j % bpp) & in_cache
            bias = jnp.where(hit, 0.0, bias)
            new_sel = jnp.maximum(new_sel, jnp.where(in_cache, 0, 1))
        s = _dot(q4_ref[0, h].astype(BF16), kbuf[slot, h].astype(BF16)) + bias
        sn = s_new + jnp.where(new_sel > 0, 0.0, NEG_INF)
        m = jnp.maximum(jnp.max(s, axis=-1, keepdims=True), sn)
        m = jnp.where(m == NEG_INF, 0.0, m)
        e, en = jnp.exp(s - m), jnp.exp(sn - m)
        den = jnp.maximum(jnp.sum(e, axis=-1, keepdims=True) + en, 1e-30)
        pv = _dot_nt(e.astype(BF16), vbuf[slot, h].astype(BF16))
        o_h = (pv + en * v_new[:, h * HEAD_DIM:(h + 1) * HEAD_DIM]) / den
        o_slc = jnp.where(rowh == h, o_h, o_slc)

    kwn = kvwn_ref[0]
    s = _dot(qbd.astype(BF16), wint_ref[0, 0:KV_WIDTH, :].astype(BF16))
    rel = win_buf - lax.broadcasted_iota(jnp.int32, (1, win_buf), 1)
    s = s + jnp.where((rel >= 0) & (rel < WINDOW), 0.0, NEG_INF)
    sn = jnp.sum(qbd * kwn[:, :KV_WIDTH], axis=-1, keepdims=True)
    m = jnp.maximum(jnp.max(s, axis=-1, keepdims=True), sn)
    e, en = jnp.exp(s - m), jnp.exp(sn - m)
    den = jnp.sum(e, axis=-1, keepdims=True) + en
    o_w = (_dot_nt(e.astype(BF16), wint_ref[0, KV_WIDTH:, :].astype(BF16)) + en * kwn[:, KV_WIDTH:]) / den

    gates = gate_ref[0]
    out_ref[0] = (gates[:, 0:1] * _heads_to_lanes64(ocmp_ref[0], rowh) + gates[:, 1:2] * o_slc
                  + gates[:, 2:3] * _heads_to_lanes64(o_w, rowh))


def _nsa_decode(page_table, idx, cache_t, qbd, q4, kvs, kvws, win_t, gate_rows, ocmp):
    nb, n_pages = page_table.shape
    page_size = cache_t.shape[2]
    assert page_size % SEL_BLOCK == 0 and page_size % LANES == 0

    def per_b(shape):
        return pl.BlockSpec((1,) + shape, lambda bb, pt, ix: (bb,) + (0,) * len(shape))

    nk = SEL_TOP_N * page_size
    grid_spec = pltpu.PrefetchScalarGridSpec(
        num_scalar_prefetch=2,
        grid=(nb,),
        in_specs=[pl.BlockSpec(memory_space=pl.ANY), per_b((ROWS, KV_WIDTH)), per_b((N_KV_HEADS, ROWS, HEAD_DIM)),
                  per_b((1, 4 * KV_WIDTH)), per_b((1, 2 * KV_WIDTH)), per_b(win_t.shape[1:]), per_b((ROWS, 3)),
                  per_b((ROWS, KV_WIDTH))],
        out_specs=per_b((ROWS, HEAD_DIM)),
        scratch_shapes=[pltpu.VMEM((2, N_KV_HEADS, HEAD_DIM, nk), F32),
                        pltpu.VMEM((2, N_KV_HEADS, HEAD_DIM, nk), F32),
                        pltpu.SemaphoreType.DMA((2,))],
    )
    return pl.pallas_call(
        functools.partial(_nsa_decode_kernel, n_pages=n_pages, page_size=page_size),
        grid_spec=grid_spec,
        out_shape=jax.ShapeDtypeStruct((nb, ROWS, HEAD_DIM), F32),
        compiler_params=_cparams(("arbitrary",)),
        name="nsa_decode",
    )(page_table.reshape(-1), idx.reshape(-1), cache_t, qbd, q4, kvs[:, None, :], kvws[:, None, :], win_t, gate_rows, ocmp)


def _pool_sample_kernel(hist_ref, z_ref, sgp_ref, wp_ref, scale_ref, out_ref, *, pos):
    z = z_ref[...]
    acc = z
    sums = {}
    for k in range(1, POOL_HIST + 1):
        acc = acc + hist_ref[POOL_HIST - k]
        if k + 1 in POOL_WINDOWS:
            sums[k + 1] = acc
    lane = lax.broadcasted_iota(jnp.int32, (1, POOL_WIDTH), 1)
    means = None
    for gi, w in enumerate(POOL_WINDOWS):
        m = sums[w] / float(min(w, pos + 1))
        means = m if means is None else jnp.where(lane >= gi * POOL_GROUP_DIM, m, means)
    mixed = _dot((means - z).astype(BF16), wp_ref[...])
    out_ref[...] = mixed * scale_ref[...] * sgp_ref[...]


def _pool_sample(hist_t, z, sgp, wp_bd, scale, pos):
    return pl.pallas_call(
        functools.partial(_pool_sample_kernel, pos=pos),
        out_shape=jax.ShapeDtypeStruct(z.shape, F32),
        name="pool_sample",
    )(hist_t, z, sgp, wp_bd, scale)


def _prep_weights(w_in, w_cmp1, b_cmp1, w_cmp2, w_pool, pool_scale, w_out):
    w1cat, b1t, w2bd = _compress_weights(w_cmp1, b_cmp1, w_cmp2)
    wp_bd = jax.scipy.linalg.block_diag(*[w_pool[g] for g in range(len(POOL_WINDOWS))]).astype(BF16)
    return dict(w_in=w_in.astype(BF16), w_out=w_out.astype(BF16), w1cat=w1cat, b1t=b1t, w2bd=w2bd,
                wp_bd=wp_bd, pool_scale=pool_scale[None, :])


def _rows_layout(x4):
    x = jnp.swapaxes(x4, 1, 2)
    pad = [(0, 0)] * x.ndim
    pad[2] = (0, 8 - N_KV_HEADS)
    x = jnp.pad(x, pad)
    return x.reshape((x.shape[0], ROWS) + x.shape[3:])


def _token_minor_to_logical(xt, lead):
    t = xt.shape[-1]
    x = xt.reshape(lead + (-1, N_KV_HEADS, HEAD_DIM, t))
    n = len(lead)
    return jnp.transpose(x, tuple(range(n)) + (n + 3, n, n + 1, n + 2))


def kernel(x_prompt, x_sample, cache_kv, state_kv_win, state_pool, page_table, ln_gain, w_in, q_norm, k_norm, w_cmp1, b_cmp1, w_cmp2, w_pool, pool_scale, w_out):
    assert cache_kv.shape[0] == 1 and x_sample.shape[1] == 1
    b, seq, _ = x_prompt.shape
    nb = x_sample.shape[0]
    n_pages, page_size = page_table.shape[1], cache_kv.shape[2]
    past = n_pages * page_size
    win_buf = state_kv_win.shape[2]
    w = _prep_weights(w_in[0], w_cmp1[0], b_cmp1[0], w_cmp2[0], w_pool[0], pool_scale[0], w_out[0])

    z, sgp, q, sgn, gates, kvt, kvwt = _project(x_prompt, jnp.arange(seq), 256, ln_gain[0], w["w_in"], q_norm[0],
                                                 k_norm[0], emit_rows=False)
    gp = _pool_prompt(z, sgp, w["wp_bd"], w["pool_scale"])
    kc, vc = _compress_prompt(kvt, w["w1cat"], w["b1t"], w["w2bd"])
    o_nsa = _nsa_prompt(q, gates, kc, vc, kvt, kvwt)
    y_p = _out_proj(x_prompt.reshape(b * seq, D_MODEL), gp.reshape(b * seq, POOL_WIDTH),
                    o_nsa.reshape(b * seq, NSA_WIDTH), sgn.reshape(b * seq, NSA_WIDTH), w["w_out"], 512)

    xs = x_sample.reshape(1, nb, D_MODEL)
    zs, sgps, qs, sgns, gates_s, kvst, kvwst, kvs, kvws = [
        a[0] for a in _project(xs, jnp.full((nb,), past), nb, ln_gain[0], w["w_in"], q_norm[0], k_norm[0], emit_rows=True)]
    cache_t = jnp.transpose(cache_kv[0], (0, 2, 3, 4, 1)).reshape(cache_kv.shape[1], 4 * KV_WIDTH, page_size)
    win_t = jnp.transpose(state_kv_win[0], (0, 2, 3, 4, 1)).reshape(nb, 2 * KV_WIDTH, win_buf)
    q4 = qs.reshape(nb, N_KV_HEADS, GQA, HEAD_DIM) * SCALE
    eye = jnp.eye(N_KV_HEADS, dtype=F32)
    qbd = _rows_layout(jnp.einsum("bhgd,hk->bhgkd", q4, eye).reshape(nb, N_KV_HEADS, GQA, KV_WIDTH))
    q4h = jnp.swapaxes(_rows_layout(jnp.einsum("bhgd,hk->bhgkd", q4, eye)), 1, 2)
    ocmp, idx = _compress_decode(page_table, cache_t, qbd, w["w1cat"], w["b1t"], w["w2bd"])
    gate_rows = _rows_layout(gates_s.reshape(nb, N_KV_HEADS, GQA, 3))
    o_rows = _nsa_decode(page_table, idx[:, :N_KV_HEADS, :], cache_t, qbd, q4h, kvs, kvws, win_t, gate_rows, ocmp)
    o_s = jnp.swapaxes(o_rows.reshape(nb, GQA, 8, HEAD_DIM)[:, :, :N_KV_HEADS], 1, 2).reshape(nb, NSA_WIDTH)
    gps = _pool_sample(jnp.swapaxes(state_pool[0], 0, 1), zs, sgps, w["wp_bd"], w["pool_scale"], past)
    y_s = _out_proj(x_sample.reshape(nb, D_MODEL), gps, o_s, sgns, w["w_out"], nb)

    kv_prompt = _token_minor_to_logical(kvt, (b,))[None]
    kv_sample = _token_minor_to_logical(kvst, ())[None, :, None]
    win_keep = min(WINDOW, seq)
    win_prompt = _token_minor_to_logical(kvwt[:, :, seq - win_keep:], (b,))[None]
    win_sample = _token_minor_to_logical(jnp.concatenate([win_t, kvws[:, :, None]], axis=2)[:, :, 1:], (nb,))[None]
    pool_prompt = z[:, seq - POOL_HIST:][None]
    pool_sample = jnp.concatenate([state_pool[0], zs[:, None, :]], axis=1)[:, 1:][None]
    return (y_p.reshape(b, seq, D_MODEL), y_s.reshape(nb, 1, D_MODEL), kv_prompt, kv_sample,
            win_prompt, win_sample, pool_prompt, pool_sample)
```

```python
import functools

import numpy as np
import jax
import jax.numpy as jnp
from jax import lax
from jax.experimental import pallas as pl
from jax.experimental.pallas import tpu as pltpu

F32 = jnp.float32
BF16 = jnp.bfloat16

D_MODEL = 1024
POOL_WIDTH = 256
POOL_WINDOWS = (2, 4, 8, 16)
POOL_GROUP_DIM = 64
POOL_HIST = 15
HEAD_DIM = 64
N_HEADS = 12
N_KV_HEADS = 4
GQA = 3
NSA_WIDTH = 768
KV_WIDTH = 256
CMP_BLOCK = 32
CMP_STRIDE = 16
SEL_BLOCK = 64
SEL_TOP_N = 16
WINDOW = 512
ROPE_DIM = 16
ROPE_THETA = 500000.0
EPS = 1e-6
FORCE_SCORE = 1e4
IN_SPLITS = (256, 256, 768, 768, 1024, 512, 36)
IN_OFFS = tuple(int(v) for v in np.cumsum((0,) + IN_SPLITS))
SCALE = HEAD_DIM ** -0.5

LANES = 128
VMEM_LIMIT_BYTES = 56 * 1024 * 1024

NEG_INF = float("-inf")
N_SLABS = 2 * KV_WIDTH // LANES


def _cparams(semantics):
    return pltpu.CompilerParams(dimension_semantics=semantics, vmem_limit_bytes=VMEM_LIMIT_BYTES)


def _dot(a, b):
    return jnp.dot(a, b, preferred_element_type=F32)


def _dot_nt(a, b):
    return lax.dot_general(a, b, (((1,), (1,)), ((), ())), preferred_element_type=F32)


def _split_dot(x, w_bf16):
    hi = x.astype(BF16)
    lo = (x - hi.astype(F32)).astype(BF16)
    return _dot(hi, w_bf16) + _dot(lo, w_bf16)


def _silu(x):
    return x * jax.nn.sigmoid(x)


def _safe_softmax(s):
    m = jnp.max(s, axis=-1, keepdims=True)
    m = jnp.where(m == NEG_INF, 0.0, m)
    e = jnp.exp(s - m)
    return e / jnp.maximum(jnp.sum(e, axis=-1, keepdims=True), 1e-30)


def _head_norm_rope(v, gain, bd, cos, sin_lo, sin_hi):
    outs = []
    for c in range(v.shape[1] // LANES):
        vc = v[:, c * LANES:(c + 1) * LANES]
        ms = _split_dot(vc * vc, bd)
        n = vc * lax.rsqrt(ms + EPS) * gain[:, c * LANES:(c + 1) * LANES]
        outs.append(n * cos + pltpu.roll(n, LANES - 8, 1) * sin_lo + pltpu.roll(n, 8, 1) * sin_hi)
    return jnp.concatenate(outs, axis=1)


def _head_norm_rope_t(vt, gain_t, cos_t, sin_t):
    outs = []
    for c in range(vt.shape[1] // LANES):
        sl = slice(c * LANES, (c + 1) * LANES)
        v = vt[:, sl].reshape(N_KV_HEADS, HEAD_DIM, LANES)
        ms = jnp.mean(v * v, axis=1, keepdims=True)
        n = v * lax.rsqrt(ms + EPS) * gain_t.reshape(N_KV_HEADS, HEAD_DIM, LANES)
        half = ROPE_DIM // 2
        x1, x2 = n[:, 0:half, :], n[:, half:ROPE_DIM, :]
        cos, sin = cos_t[:, sl][None], sin_t[:, sl][None]
        r = jnp.concatenate([x1 * cos - x2 * sin, x2 * cos + x1 * sin, n[:, ROPE_DIM:, :]], axis=1)
        outs.append(r.reshape(N_KV_HEADS * HEAD_DIM, LANES))
    return jnp.concatenate(outs, axis=1)


def _project_kernel(x_ref, gain_ref, w_ref, wt_ref, qg_ref, kgt_ref, bd_ref, cos_ref, slo_ref, shi_ref, cost_ref, sint_ref,
                    z_ref, sgp_ref, q_ref, sgn_ref, gate_ref, kvt_ref, kvwt_ref, *row_refs):
    x = x_ref[0]
    ms = jnp.mean(x * x, axis=-1, keepdims=True)
    hb = (x * lax.rsqrt(ms + EPS) * gain_ref[...]).astype(BF16)

    def proj(lo, hi):
        return _dot(hb, w_ref[:, lo:hi])

    o = IN_OFFS
    z_ref[0] = proj(o[0], o[1])
    sgp_ref[0] = _silu(proj(o[1], o[2]))
    q_ref[0] = _head_norm_rope(proj(o[2], o[3]), qg_ref[...], bd_ref[...], cos_ref[...], slo_ref[...], shi_ref[...])
    sgn_ref[0] = _silu(proj(o[3], o[4]))
    gate_ref[0] = jax.nn.sigmoid(proj(o[6], o[7]))
    key_gain = {0: 0, 2: 1, 4: 2}
    cos_t, sin_t = cost_ref[...], sint_ref[...]
    for grp in range(6):
        val_t = _dot_nt(wt_ref[grp * KV_WIDTH:(grp + 1) * KV_WIDTH, :], hb)
        if grp in key_gain:
            val_t = _head_norm_rope_t(val_t, kgt_ref[key_gain[grp]], cos_t, sin_t)
        dst, r0 = (kvt_ref, grp * KV_WIDTH) if grp < 4 else (kvwt_ref, (grp - 4) * KV_WIDTH)
        dst[0, r0:r0 + KV_WIDTH, :] = val_t
        if row_refs:
            row_refs[0 if grp < 4 else 1][0, :, r0:r0 + KV_WIDTH] = val_t.T


def _rope_tables(pos):
    half = ROPE_DIM // 2
    freqs = ROPE_THETA ** (-jnp.arange(half, dtype=F32) / half)
    ang = pos.astype(F32)[:, None] * freqs[None, :]
    cos8, sin8 = jnp.cos(ang), jnp.sin(ang)
    p = pos.shape[0]
    rest = HEAD_DIM - ROPE_DIM
    cos = jnp.concatenate([cos8, cos8, jnp.ones((p, rest), F32)], axis=1)
    slo = jnp.concatenate([-sin8, jnp.zeros((p, HEAD_DIM - half), F32)], axis=1)
    shi = jnp.concatenate([jnp.zeros((p, half), F32), sin8, jnp.zeros((p, rest), F32)], axis=1)
    return tuple(jnp.tile(t, (1, LANES // HEAD_DIM)) for t in (cos, slo, shi)) + (cos8.T, sin8.T)


def _project(x, pos, tm, ln_gain, w_bf16, q_norm, k_norm, emit_rows):
    b, s, _ = x.shape
    assert pos.shape[0] == s and tm % LANES == 0
    cos, slo, shi, cos_t, sin_t = _rope_tables(pos)
    tab_spec = pl.BlockSpec((tm, LANES), lambda bb, i: (i, 0))
    tab_t_spec = pl.BlockSpec((ROPE_DIM // 2, tm), lambda bb, i: (0, i))
    bd = jnp.asarray(np.kron(np.eye(LANES // HEAD_DIM), np.full((HEAD_DIM, HEAD_DIM), 1.0 / HEAD_DIM)), BF16)
    qg = jnp.tile(q_norm, N_HEADS)[None, :]
    kgt = jnp.broadcast_to(jnp.tile(k_norm, (1, N_KV_HEADS))[:, :, None], (k_norm.shape[0], KV_WIDTH, LANES))
    gain = ln_gain[None, :]
    wt = w_bf16[:, IN_OFFS[4]:IN_OFFS[6]].T

    def full(a):
        return pl.BlockSpec(a.shape, lambda bb, i: (0,) * a.ndim)

    def rows(w):
        return pl.BlockSpec((1, tm, w), lambda bb, i: (bb, i, 0))

    def cols(f):
        return pl.BlockSpec((1, f, tm), lambda bb, i: (bb, 0, i))

    row_w = (256, 256, 768, 768, 36)
    out_specs = [rows(w) for w in row_w] + [cols(4 * KV_WIDTH), cols(2 * KV_WIDTH)]
    out_shape = ([jax.ShapeDtypeStruct((b, s, w), F32) for w in row_w]
                 + [jax.ShapeDtypeStruct((b, 4 * KV_WIDTH, s), F32), jax.ShapeDtypeStruct((b, 2 * KV_WIDTH, s), F32)])
    if emit_rows:
        out_specs += [rows(4 * KV_WIDTH), rows(2 * KV_WIDTH)]
        out_shape += [jax.ShapeDtypeStruct((b, s, 4 * KV_WIDTH), F32), jax.ShapeDtypeStruct((b, s, 2 * KV_WIDTH), F32)]
    return pl.pallas_call(
        _project_kernel,
        grid=(b, s // tm),
        in_specs=[rows(D_MODEL), full(gain), full(w_bf16), full(wt), full(qg), full(kgt), full(bd),
                  tab_spec, tab_spec, tab_spec, tab_t_spec, tab_t_spec],
        out_specs=out_specs,
        out_shape=out_shape,
        compiler_params=_cparams(("parallel", "parallel")),
        name="project",
    )(x, gain, w_bf16, wt, qg, kgt, bd, cos, slo, shi, cos_t, sin_t)


def _pool_prompt_kernel(z_ref, sgp_ref, wp_ref, scale_ref, out_ref, pad_ref):
    length = z_ref.shape[1]
    pad_ref[0:16, :] = jnp.zeros((16, POOL_WIDTH), F32)
    pad_ref[16:16 + length, :] = z_ref[0]
    z = pad_ref[16:16 + length, :]
    acc = z
    sums = {}
    for k in range(1, POOL_HIST + 1):
        acc = acc + pad_ref[16 - k:16 - k + length, :]
        if k + 1 in POOL_WINDOWS:
            sums[k + 1] = acc
    pos = lax.broadcasted_iota(jnp.int32, (length, 1), 0)
    lane = lax.broadcasted_iota(jnp.int32, (1, POOL_WIDTH), 1)
    means = None
    for gi, w in enumerate(POOL_WINDOWS):
        m = sums[w] / jnp.minimum(w, pos + 1).astype(F32)
        means = m if means is None else jnp.where(lane >= gi * POOL_GROUP_DIM, m, means)
    mixed = _dot((means - z).astype(BF16), wp_ref[...])
    out_ref[0] = mixed * scale_ref[...] * sgp_ref[0]


def _pool_prompt(z, sgp, wp_bd, scale):
    b, length, _ = z.shape
    blk = pl.BlockSpec((1, length, POOL_WIDTH), lambda i: (i, 0, 0))
    return pl.pallas_call(
        _pool_prompt_kernel,
        grid=(b,),
        in_specs=[blk, blk, pl.BlockSpec(wp_bd.shape, lambda i: (0, 0)), pl.BlockSpec(scale.shape, lambda i: (0, 0))],
        out_specs=blk,
        out_shape=jax.ShapeDtypeStruct(z.shape, F32),
        scratch_shapes=[pltpu.VMEM((16 + length, POOL_WIDTH), F32)],
        compiler_params=_cparams(("parallel",)),
        name="pool_prompt",
    )(z, sgp, wp_bd, scale)


def _compress_accumulate(load_rows, w1_ref, slot, nj):
    acc = jnp.zeros((nj, 2 * LANES), F32)
    for s2 in range(CMP_STRIDE // 2):
        lhs = jnp.concatenate([load_rows(2 * s2), load_rows(2 * s2 + 1)], axis=1).astype(BF16)
        acc = acc + _dot(lhs, w1_ref[slot, s2])
    return acc


def _compress_finish(acc, b1, w2):
    nj = acc.shape[0]
    second = pltpu.roll(acc[:, LANES:], nj - 1, 0)
    hid = _silu(acc[:, :LANES] + second + b1)
    return _dot(hid.astype(BF16), w2)


def _compress_prompt_kernel(kvt_ref, w1_ref, b1_ref, w2_ref, kc_ref, vc_ref, xbuf):
    length = kvt_ref.shape[2]
    nj = length // CMP_STRIDE
    for sl in range(N_SLABS):
        for tt in range(length // LANES):
            xbuf[sl, tt * LANES:(tt + 1) * LANES, :] = kvt_ref[0, sl * LANES:(sl + 1) * LANES, tt * LANES:(tt + 1) * LANES].T
    for sl in range(N_SLABS):
        slot, out_ref = sl // 2, (kc_ref, vc_ref)[sl // 2]
        acc = _compress_accumulate(lambda s, sl=sl: xbuf[sl, pl.ds(s, nj, stride=CMP_STRIDE), :], w1_ref, slot, nj)
        out_ref[0, :, (sl % 2) * LANES:(sl % 2 + 1) * LANES] = _compress_finish(acc, b1_ref[slot], w2_ref[slot])


def _compress_prompt(kvt, w1cat, b1t, w2bd):
    b, _, length = kvt.shape
    nj = length // CMP_STRIDE

    def full(a):
        return pl.BlockSpec(a.shape, lambda bb: (0,) * a.ndim)

    out_blk = pl.BlockSpec((1, nj, KV_WIDTH), lambda bb: (bb, 0, 0))
    return pl.pallas_call(
        _compress_prompt_kernel,
        grid=(b,),
        in_specs=[pl.BlockSpec((1, 2 * KV_WIDTH, length), lambda bb: (bb, 0, 0)), full(w1cat), full(b1t), full(w2bd)],
        out_specs=[out_blk, out_blk],
        out_shape=[jax.ShapeDtypeStruct((b, nj, KV_WIDTH), F32)] * 2,
        scratch_shapes=[pltpu.VMEM((N_SLABS, length, LANES), F32)],
        compiler_params=_cparams(("parallel",)),
        name="compress_prompt",
    )(kvt, w1cat, b1t, w2bd)


def _compress_weights(w_cmp1, b_cmp1, w_cmp2):
    eye2 = jnp.eye(LANES // HEAD_DIM, dtype=F32)

    def bd2(m):
        return jnp.kron(eye2, m)

    w1cat = []
    for slot in range(2):
        per = []
        for s2 in range(CMP_STRIDE // 2):
            blocks = []
            for s in (2 * s2, 2 * s2 + 1):
                blocks.append(jnp.concatenate([bd2(w_cmp1[slot, s]), bd2(w_cmp1[slot, CMP_STRIDE + s])], axis=1))
            per.append(jnp.concatenate(blocks, axis=0))
        w1cat.append(jnp.stack(per))
    w1cat = jnp.stack(w1cat).astype(BF16)
    b1t = jnp.tile(b_cmp1, (1, LANES // HEAD_DIM))[:, None, :]
    w2bd = jnp.stack([bd2(w_cmp2[0]), bd2(w_cmp2[1])]).astype(BF16)
    return w1cat, b1t, w2bd


def _overlap_matrix(n_cmp_pad, n_cmp, n_sel, n_sel_pad):
    cstart = np.arange(n_cmp_pad)[:, None] * CMP_STRIDE
    bstart = np.arange(n_sel_pad)[None, :] * SEL_BLOCK
    ov = np.clip(np.minimum(cstart + CMP_BLOCK, bstart + SEL_BLOCK) - np.maximum(cstart, bstart), 0, None) / CMP_BLOCK
    ov = ov * (np.arange(n_cmp_pad)[:, None] < n_cmp) * (np.arange(n_sel_pad)[None, :] < n_sel)
    return jnp.asarray(ov, BF16)


def _topk_mask(score, n_iter):
    lane = lax.broadcasted_iota(jnp.int32, score.shape, 1)
    rank = jnp.zeros(score.shape, F32)
    for j in range(n_iter):
        col = score[:, j:j + 1]
        ahead = (col > score) | ((col == score) & (lane > j))
        rank = rank + jnp.where(ahead, 1.0, 0.0)
    return rank, (rank < SEL_TOP_N) & (score > NEG_INF)


TQ = 128
WIN_TILES = WINDOW // TQ + 1
Q_GROUPS = 4


def _topk_mask_t(score_t):
    row = lax.broadcasted_iota(jnp.int32, score_t.shape, 0)
    rank = jnp.zeros(score_t.shape, F32)
    for j in range(score_t.shape[0]):
        cur = score_t[j:j + 1, :]
        ahead = (cur > score_t) | ((cur == score_t) & (row > j))
        rank = rank + jnp.where(ahead, 1.0, 0.0)
    return (rank < SEL_TOP_N) & (score_t > NEG_INF)


def _attend_post(s, v_t):
    m = jnp.max(s, axis=-1, keepdims=True)
    m = jnp.where(m == NEG_INF, 0.0, m)
    e = jnp.exp(s - m)
    den = jnp.maximum(jnp.sum(e, axis=-1, keepdims=True), 1e-30)
    return _dot_nt(e.astype(BF16), v_t) / den


def _nsa_prompt_kernel(*refs, tile0):
    q_ref, gate_ref, kc_ref, vc_ref, kst_ref, vst_ref = refs[:6]
    win_refs, (ovt_ref, ex_ref, out_ref) = refs[6:6 + WIN_TILES], refs[6 + WIN_TILES:]
    i = tile0 + pl.program_id(1)
    nk = kst_ref.shape[2]
    n_cmp = kc_ref.shape[1]
    n_sel = ex_ref.shape[0]
    qpos = i * TQ + lax.broadcasted_iota(jnp.int32, (TQ, 1), 0)
    qpos_l = i * TQ + lax.broadcasted_iota(jnp.int32, (1, TQ), 1)
    lane = lax.broadcasted_iota(jnp.int32, (1, LANES), 1)
    lane_half = lane // HEAD_DIM

    ncol = lax.broadcasted_iota(jnp.int32, (1, n_cmp), 1)
    bias_c = jnp.where(ncol * CMP_STRIDE + CMP_BLOCK - 1 <= qpos, 0.0, NEG_INF)
    bias_w = []
    for u in range(WIN_TILES):
        kb = i - (WIN_TILES - 1) + u
        rel = qpos - (kb * TQ + lane)
        bias_w.append(jnp.where((kb >= 0) & (rel >= 0) & (rel < WINDOW), 0.0, NEG_INF))
    bias_w = jnp.concatenate(bias_w, axis=1)
    jrow = lax.broadcasted_iota(jnp.int32, (n_sel, 1), 0)
    qblk_l = qpos_l // SEL_BLOCK
    valid_t = jrow * SEL_BLOCK <= qpos_l
    forced_t = (jrow == 0) | (jrow == qblk_l) | (jrow == qblk_l - 1)
    causal = lax.broadcasted_iota(jnp.int32, (1, nk), 1) <= qpos
    gates = gate_ref[0]

    def add_bias(s, bias):
        return (s.reshape(GQA, TQ, s.shape[-1]) + bias[None]).reshape(GQA * TQ, s.shape[-1])

    out_chunks = [jnp.zeros((TQ, LANES), F32) for _ in range(NSA_WIDTH // LANES)]
    for h in range(N_KV_HEADS):
        pair, half = h // 2, h % 2
        lo, hi = pair * LANES, (pair + 1) * LANES
        rows = []
        for g in range(GQA):
            qh = h * GQA + g
            x = q_ref[0, :, (qh // 2) * LANES:(qh // 2 + 1) * LANES] * SCALE
            if qh % 2 != half:
                x = pltpu.roll(x, HEAD_DIM, 1)
            rows.append(jnp.where(lane_half == half, x, 0.0))
        qz = jnp.concatenate(rows, axis=0).astype(BF16)

        p_c = _safe_softmax(add_bias(_dot_nt(qz, kc_ref[0, :, lo:hi].astype(BF16)), bias_c))
        o_cmp = _dot(p_c.astype(BF16), vc_ref[0, :, lo:hi].astype(BF16))
        p_sum = p_c[0:TQ] + p_c[TQ:2 * TQ] + p_c[2 * TQ:3 * TQ]
        p_hi = p_sum.astype(BF16)
        p_lo = (p_sum - p_hi.astype(F32)).astype(BF16)
        imp_t = _dot_nt(ovt_ref[...], p_hi) + _dot_nt(ovt_ref[...], p_lo)
        score_t = jnp.where(valid_t, jnp.where(forced_t, FORCE_SCORE, imp_t), NEG_INF)
        sel = jnp.where(_topk_mask_t(score_t), 1.0, 0.0).T.astype(BF16)

        selw = _dot(sel, ex_ref[...])
        bias_s = jnp.where((selw > 0.5) & causal, 0.0, NEG_INF)
        o_slc = _attend_post(add_bias(_dot(qz, kst_ref[0, lo:hi, :].astype(BF16)), bias_s),
                             vst_ref[0, lo:hi, :].astype(BF16))
        kwt = jnp.concatenate([w[0, lo:hi, :] for w in win_refs], axis=1).astype(BF16)
        vwt = jnp.concatenate([w[0, KV_WIDTH + lo:KV_WIDTH + hi, :] for w in win_refs], axis=1).astype(BF16)
        o_win = _attend_post(add_bias(_dot(qz, kwt), bias_w), vwt)

        for g in range(GQA):
            qh = h * GQA + g
            gi = qh * 3
            sl = slice(g * TQ, (g + 1) * TQ)
            o = (gates[:, gi:gi + 1] * o_cmp[sl] + gates[:, gi + 1:gi + 2] * o_slc[sl] + gates[:, gi + 2:gi + 3] * o_win[sl])
            if qh % 2 != half:
                o = pltpu.roll(o, HEAD_DIM, 1)
            out_chunks[qh // 2] = out_chunks[qh // 2] + jnp.where(lane_half == qh % 2, o, 0.0)
    for c, val in enumerate(out_chunks):
        out_ref[0, :, c * LANES:(c + 1) * LANES] = val


def _nsa_prompt(q, gates, kc, vc, kvt, kvwt):
    b, seq, _ = q.shape
    n_cmp = kc.shape[1]
    n_sel = seq // SEL_BLOCK
    n_tiles = seq // TQ
    assert n_sel % 8 == 0 and n_tiles % Q_GROUPS == 0 and (seq // Q_GROUPS) % LANES == 0
    ovt = _overlap_matrix(n_cmp, (seq - CMP_BLOCK) // CMP_STRIDE + 1, n_sel, n_sel).T
    ex = np.arange(n_sel)[:, None] == (np.arange(seq)[None, :] // SEL_BLOCK)
    tiles_per_group = n_tiles // Q_GROUPS

    outs = []
    for grp in range(Q_GROUPS):
        tile0 = grp * tiles_per_group
        nk = (grp + 1) * (seq // Q_GROUPS)
        ex_g = jnp.asarray(ex[:, :nk], BF16)

        def tile(w, tile0=tile0):
            return pl.BlockSpec((1, TQ, w), lambda bb, i: (bb, tile0 + i, 0))

        def whole(rows):
            return pl.BlockSpec((1, rows, KV_WIDTH), lambda bb, i: (bb, 0, 0))

        def feat_rows(row_blk, nk=nk):
            return pl.BlockSpec((1, KV_WIDTH, nk), lambda bb, i, rb=row_blk: (bb, rb, 0))

        def win_tile(u, tile0=tile0):
            return pl.BlockSpec((1, 2 * KV_WIDTH, TQ),
                                lambda bb, i, u=u: (bb, 0, jnp.maximum(tile0 + i - (WIN_TILES - 1) + u, 0)))

        def full(a):
            return pl.BlockSpec(a.shape, lambda bb, i: (0,) * a.ndim)

        in_specs = ([tile(NSA_WIDTH), tile(gates.shape[-1]), whole(n_cmp), whole(n_cmp), feat_rows(2), feat_rows(3)]
                    + [win_tile(u) for u in range(WIN_TILES)] + [full(ovt), full(ex_g)])
        args = [q, gates, kc, vc, kvt, kvt] + [kvwt] * WIN_TILES + [ovt, ex_g]
        outs.append(pl.pallas_call(
            functools.partial(_nsa_prompt_kernel, tile0=tile0),
            grid=(b, tiles_per_group),
            in_specs=in_specs,
            out_specs=pl.BlockSpec((1, TQ, NSA_WIDTH), lambda bb, i: (bb, i, 0)),
            out_shape=jax.ShapeDtypeStruct((b, seq // Q_GROUPS, NSA_WIDTH), F32),
            compiler_params=_cparams(("parallel", "arbitrary")),
            name=f"nsa_prompt_g{grp}",
        )(*args))
    return jnp.concatenate(outs, axis=1)


def _out_proj_kernel(x_ref, gp_ref, o_ref, sgn_ref, w_ref, y_ref):
    gn = (o_ref[...] * sgn_ref[...]).astype(BF16)
    y_ref[...] = (x_ref[...] + _dot(gp_ref[...].astype(BF16), w_ref[0:POOL_WIDTH, :])
                  + _dot(gn, w_ref[POOL_WIDTH:, :]))


def _out_proj(x, gp, o_nsa, sgn, w_bf16, tm):
    t = x.shape[0]

    def rows(w):
        return pl.BlockSpec((tm, w), lambda i: (i, 0))

    return pl.pallas_call(
        _out_proj_kernel,
        grid=(t // tm,),
        in_specs=[rows(D_MODEL), rows(POOL_WIDTH), rows(NSA_WIDTH), rows(NSA_WIDTH),
                  pl.BlockSpec(w_bf16.shape, lambda i: (0, 0))],
        out_specs=rows(D_MODEL),
        out_shape=jax.ShapeDtypeStruct((t, D_MODEL), F32),
        compiler_params=_cparams(("parallel",)),
        name="out_proj",
    )(x, gp, o_nsa, sgn, w_bf16)


ROWS = GQA * 8
CH_PAGES = 32


def _page_copy(cache_ref, xt, sem, page, p, slot):
    return pltpu.make_async_copy(cache_ref.at[page, pl.ds(0, 2 * KV_WIDTH), :], xt.at[slot, p], sem.at[slot])


def _compress_decode_kernel(pt_ref, cache_ref, qbd_ref, perm_ref, w1_ref, b1_ref, w2_ref, ov_ref, ocmp_ref, idx_ref,
                            xt, xs, fg, sem, *, n_pages, page_size):
    b, c = pl.program_id(0), pl.program_id(1)
    nb, nch = pl.num_programs(0), pl.num_programs(1)
    t = b * nch + c
    bpp = page_size // CMP_STRIDE
    jch = CH_PAGES * bpp
    nj = nch * jch
    past = n_pages * page_size

    def start_chunk(step, slot):
        base = (step // nch) * n_pages + (step % nch) * CH_PAGES

        def body(p, carry):
            _page_copy(cache_ref, xt, sem, pt_ref[base + p], p, slot).start()
            return carry
        lax.fori_loop(0, CH_PAGES, body, 0)

    @pl.when(t == 0)
    def _():
        start_chunk(0, 0)

    @pl.when(t + 1 < nb * nch)
    def _():
        start_chunk(t + 1, (t + 1) % 2)

    slot = t % 2

    def wait_body(p, carry):
        _page_copy(cache_ref, xt, sem, 0, p, slot).wait()
        return carry
    lax.fori_loop(0, CH_PAGES, wait_body, 0)

    def stage_body(pp, carry):
        ys = [_dot_nt(perm_ref[...], xt[slot, 2 * pp + k].astype(BF16)) for k in range(2)]
        r0 = pl.multiple_of(pp * 2 * bpp, 2 * bpp)
        for s in range(CMP_STRIDE):
            rows = jnp.concatenate([y[s * bpp:(s + 1) * bpp, :] for y in ys], axis=0)
            xs[s, pl.ds(r0, 2 * bpp), :] = rows.astype(BF16)
        return carry
    lax.fori_loop(0, CH_PAGES // 2, stage_body, 0, unroll=8)

    row0 = pl.multiple_of(c * jch, jch)
    for sl in range(N_SLABS):
        acc = jnp.zeros((jch, 2 * LANES), F32)
        for s2 in range(CMP_STRIDE // 2):
            lhs = jnp.concatenate([xs[2 * s2, :, sl * LANES:(sl + 1) * LANES],
                                   xs[2 * s2 + 1, :, sl * LANES:(sl + 1) * LANES]], axis=1)
            acc = acc + _dot(lhs, w1_ref[sl // 2, s2])
        fg[sl, pl.ds(row0, jch), :] = acc

    @pl.when(c == nch - 1)
    def _():
        kcv = [_compress_finish(fg[sl], b1_ref[sl // 2], w2_ref[sl // 2]) for sl in range(N_SLABS)]
        kc = jnp.concatenate(kcv[0:2], axis=1).astype(BF16)
        vc = jnp.concatenate(kcv[2:4], axis=1).astype(BF16)
        s = _dot_nt(qbd_ref[0].astype(BF16), kc)
        ncol = lax.broadcasted_iota(jnp.int32, (1, nj), 1)
        s = s + jnp.where(ncol * CMP_STRIDE + CMP_BLOCK - 1 <= past, 0.0, NEG_INF)
        p = _safe_softmax(s)
        ocmp_ref[0] = _dot(p.astype(BF16), vc)
        p_sum = p[0:8] + p[8:16] + p[16:24]
        imp = _split_dot(p_sum, ov_ref[...])
        lane = lax.broadcasted_iota(jnp.int32, (1, imp.shape[1]), 1)
        qblk = past // SEL_BLOCK
        forced = (lane == 0) | (lane == qblk) | (lane == qblk - 1)
        score = jnp.where(lane * SEL_BLOCK <= past, jnp.where(forced, FORCE_SCORE, imp), NEG_INF)
        rank, sel = _topk_mask(score, qblk + 1)
        lanef = lane.astype(F32)
        lane16 = lax.broadcasted_iota(jnp.int32, (1, SEL_TOP_N), 1)
        idx = jnp.zeros((8, SEL_TOP_N), F32)
        for r in range(SEL_TOP_N):
            v = jnp.sum(jnp.where((rank == r) & sel, lanef, 0.0), axis=-1, keepdims=True)
            idx = jnp.where(lane16 == r, v, idx)
        idx_ref[0] = idx.astype(jnp.int32)


def _compress_decode(page_table, cache_t, qbd, w1cat, b1t, w2bd):
    nb, n_pages = page_table.shape
    page_size = cache_t.shape[2]
    assert n_pages % CH_PAGES == 0 and page_size % LANES == 0
    nch = n_pages // CH_PAGES
    past = n_pages * page_size
    nj = past // CMP_STRIDE
    n_sel = past // SEL_BLOCK + 1
    assert n_sel > SEL_TOP_N
    n_sel_pad = -(-n_sel // LANES) * LANES
    ov = _overlap_matrix(nj, (past + 1 - CMP_BLOCK) // CMP_STRIDE + 1, n_sel, n_sel_pad)
    bpp = page_size // CMP_STRIDE
    tok = np.arange(page_size)
    perm = jnp.asarray(((tok % CMP_STRIDE) * bpp + tok // CMP_STRIDE)[None, :] == np.arange(page_size)[:, None], BF16)

    def full(a):
        return pl.BlockSpec(a.shape, lambda bb, cc, pt: (0,) * a.ndim)

    def per_b(shape):
        return pl.BlockSpec((1,) + shape, lambda bb, cc, pt: (bb, 0, 0))

    grid_spec = pltpu.PrefetchScalarGridSpec(
        num_scalar_prefetch=1,
        grid=(nb, nch),
        in_specs=[pl.BlockSpec(memory_space=pl.ANY), per_b((ROWS, KV_WIDTH)), full(perm), full(w1cat), full(b1t),
                  full(w2bd), full(ov)],
        out_specs=[per_b((ROWS, KV_WIDTH)), per_b((8, SEL_TOP_N))],
        scratch_shapes=[pltpu.VMEM((2, CH_PAGES, 2 * KV_WIDTH, page_size), F32),
                        pltpu.VMEM((CMP_STRIDE, CH_PAGES * bpp, 2 * KV_WIDTH), BF16),
                        pltpu.VMEM((N_SLABS, nj, 2 * LANES), F32),
                        pltpu.SemaphoreType.DMA((2,))],
    )
    return pl.pallas_call(
        functools.partial(_compress_decode_kernel, n_pages=n_pages, page_size=page_size),
        grid_spec=grid_spec,
        out_shape=[jax.ShapeDtypeStruct((nb, ROWS, KV_WIDTH), F32), jax.ShapeDtypeStruct((nb, 8, SEL_TOP_N), jnp.int32)],
        compiler_params=_cparams(("arbitrary", "arbitrary")),
        name="compress_decode",
    )(page_table.reshape(-1), cache_t, qbd, perm, w1cat, b1t, w2bd, ov)


def _block_copy(cache_ref, kvbuf, sem, pt_ref, idx_ref, bb, h, r, slot, n_pages, page_size):
    bpp = page_size // SEL_BLOCK
    j = idx_ref[(bb * N_KV_HEADS + h) * SEL_TOP_N + r]
    jj = jnp.minimum(j, n_pages * bpp - 1)
    page = pt_ref[bb * n_pages + jj // bpp]
    dst0 = pl.multiple_of(r * page_size, page_size)
    return pltpu.make_async_copy(cache_ref.at[page, pl.ds(2, 2), h],
                                 kvbuf.at[slot, h, :, :, pl.ds(dst0, page_size)], sem.at[slot])


def _heads_to_lanes64(x, rowh):
    out = jnp.zeros((x.shape[0], HEAD_DIM), F32)
    for h in range(N_KV_HEADS):
        out = jnp.where(rowh == h, x[:, h * HEAD_DIM:(h + 1) * HEAD_DIM], out)
    return out


def _nsa_decode_kernel(pt_ref, idx_ref, cache_ref, qbd_ref, q4_ref, kvn_ref, kvwn_ref, wint_ref, gate_ref, ocmp_ref,
                       out_ref, kvbuf, sem, *, n_pages, page_size):
    b = pl.program_id(0)
    nb = pl.num_programs(0)
    bpp = page_size // SEL_BLOCK
    n_past_blocks = n_pages * bpp
    nk = SEL_TOP_N * page_size
    win_buf = wint_ref.shape[2]

    def gather(bb, slot, wait):
        for h in range(N_KV_HEADS):
            def body(r, carry, h=h):
                cp = _block_copy(cache_ref, kvbuf, sem, pt_ref, idx_ref, bb, h, r, slot, n_pages, page_size)
                cp.wait() if wait else cp.start()
                return carry
            lax.fori_loop(0, SEL_TOP_N, body, 0)

    @pl.when(b == 0)
    def _():
        gather(0, 0, wait=False)

    nxt, nslot = (b + 1) % nb, (b + 1) % 2
    slot = b % 2
    gather(b, slot, wait=True)

    qbd = qbd_ref[0]
    rowh = lax.broadcasted_iota(jnp.int32, (ROWS, 1), 0) % 8
    kvn = kvn_ref[0]
    k_new, v_new = kvn[:, 2 * KV_WIDTH:3 * KV_WIDTH], kvn[:, 3 * KV_WIDTH:]
    col = lax.broadcasted_iota(jnp.int32, (1, nk), 1)
    col_r, col_blk = col // page_size, (col % page_size) // SEL_BLOCK

    s_new = jnp.sum(qbd * k_new, axis=-1, keepdims=True)
    o_slc = jnp.zeros((ROWS, HEAD_DIM), F32)
    for h in range(N_KV_HEADS):
        bias = jnp.full((1, nk), NEG_INF, F32)
        new_sel = jnp.zeros((1, 1), jnp.int32)
        for r in range(SEL_TOP_N):
            j = idx_ref[(b * N_KV_HEADS + h) * SEL_TOP_N + r]
            in_cache = j < n_past_blocks
            hit = (col_r == r) & (col_blk == j % bpp) & in_cache
            bias = jnp.where(hit, 0.0, bias)
            new_sel = jnp.maximum(new_sel, jnp.where(in_cache, 0, 1))
        s = _dot(q4_ref[0, h].astype(BF16), kvbuf[slot, h, 0].astype(BF16)) + bias
        sn = s_new + jnp.where(new_sel > 0, 0.0, NEG_INF)
        m = jnp.maximum(jnp.max(s, axis=-1, keepdims=True), sn)
        m = jnp.where(m == NEG_INF, 0.0, m)
        e, en = jnp.exp(s - m), jnp.exp(sn - m)
        den = jnp.maximum(jnp.sum(e, axis=-1, keepdims=True) + en, 1e-30)
        pv = _dot_nt(e.astype(BF16), kvbuf[slot, h, 1].astype(BF16))
        o_h = (pv + en * v_new[:, h * HEAD_DIM:(h + 1) * HEAD_DIM]) / den
        o_slc = jnp.where(rowh == h, o_h, o_slc)
        for r in range(SEL_TOP_N):
            _block_copy(cache_ref, kvbuf, sem, pt_ref, idx_ref, nxt, h, r, nslot, n_pages, page_size).start()

    kwn = kvwn_ref[0]
    s = _dot(qbd.astype(BF16), wint_ref[0, 0:KV_WIDTH, :].astype(BF16))
    rel = win_buf - lax.broadcasted_iota(jnp.int32, (1, win_buf), 1)
    s = s + jnp.where((rel >= 0) & (rel < WINDOW), 0.0, NEG_INF)
    sn = jnp.sum(qbd * kwn[:, :KV_WIDTH], axis=-1, keepdims=True)
    m = jnp.maximum(jnp.max(s, axis=-1, keepdims=True), sn)
    e, en = jnp.exp(s - m), jnp.exp(sn - m)
    den = jnp.sum(e, axis=-1, keepdims=True) + en
    o_w = (_dot_nt(e.astype(BF16), wint_ref[0, KV_WIDTH:, :].astype(BF16)) + en * kwn[:, KV_WIDTH:]) / den

    gates = gate_ref[0]
    out_ref[0] = (gates[:, 0:1] * _heads_to_lanes64(ocmp_ref[0], rowh) + gates[:, 1:2] * o_slc
                  + gates[:, 2:3] * _heads_to_lanes64(o_w, rowh))

    @pl.when(b == nb - 1)
    def _():
        gather(nxt, nslot, wait=True)


def _nsa_decode(page_table, idx, cache_t, qbd, q4, kvs, kvws, win_t, gate_rows, ocmp):
    nb, n_pages = page_table.shape
    page_size = cache_t.shape[2]
    assert page_size % SEL_BLOCK == 0 and page_size % LANES == 0
    cache5 = cache_t.reshape(cache_t.shape[0], 4, N_KV_HEADS, HEAD_DIM, page_size)

    def per_b(shape):
        return pl.BlockSpec((1,) + shape, lambda bb, pt, ix: (bb,) + (0,) * len(shape))

    nk = SEL_TOP_N * page_size
    grid_spec = pltpu.PrefetchScalarGridSpec(
        num_scalar_prefetch=2,
        grid=(nb,),
        in_specs=[pl.BlockSpec(memory_space=pl.ANY), per_b((ROWS, KV_WIDTH)), per_b((N_KV_HEADS, ROWS, HEAD_DIM)),
                  per_b((1, 4 * KV_WIDTH)), per_b((1, 2 * KV_WIDTH)), per_b(win_t.shape[1:]), per_b((ROWS, 3)),
                  per_b((ROWS, KV_WIDTH))],
        out_specs=per_b((ROWS, HEAD_DIM)),
        scratch_shapes=[pltpu.VMEM((2, N_KV_HEADS, 2, HEAD_DIM, nk), F32),
                        pltpu.SemaphoreType.DMA((2,))],
    )
    return pl.pallas_call(
        functools.partial(_nsa_decode_kernel, n_pages=n_pages, page_size=page_size),
        grid_spec=grid_spec,
        out_shape=jax.ShapeDtypeStruct((nb, ROWS, HEAD_DIM), F32),
        compiler_params=_cparams(("arbitrary",)),
        name="nsa_decode",
    )(page_table.reshape(-1), idx.reshape(-1), cache5, qbd, q4, kvs[:, None, :], kvws[:, None, :], win_t, gate_rows, ocmp)


def _pool_sample_kernel(hist_ref, z_ref, sgp_ref, wp_ref, scale_ref, out_ref, *, pos):
    z = z_ref[...]
    acc = z
    sums = {}
    for k in range(1, POOL_HIST + 1):
        acc = acc + hist_ref[POOL_HIST - k]
        if k + 1 in POOL_WINDOWS:
            sums[k + 1] = acc
    lane = lax.broadcasted_iota(jnp.int32, (1, POOL_WIDTH), 1)
    means = None
    for gi, w in enumerate(POOL_WINDOWS):
        m = sums[w] / float(min(w, pos + 1))
        means = m if means is None else jnp.where(lane >= gi * POOL_GROUP_DIM, m, means)
    mixed = _dot((means - z).astype(BF16), wp_ref[...])
    out_ref[...] = mixed * scale_ref[...] * sgp_ref[...]


def _pool_sample(hist_t, z, sgp, wp_bd, scale, pos):
    return pl.pallas_call(
        functools.partial(_pool_sample_kernel, pos=pos),
        out_shape=jax.ShapeDtypeStruct(z.shape, F32),
        name="pool_sample",
    )(hist_t, z, sgp, wp_bd, scale)


def _prep_weights(w_in, w_cmp1, b_cmp1, w_cmp2, w_pool, pool_scale, w_out):
    w1cat, b1t, w2bd = _compress_weights(w_cmp1, b_cmp1, w_cmp2)
    wp_bd = jax.scipy.linalg.block_diag(*[w_pool[g] for g in range(len(POOL_WINDOWS))]).astype(BF16)
    return dict(w_in=w_in.astype(BF16), w_out=w_out.astype(BF16), w1cat=w1cat, b1t=b1t, w2bd=w2bd,
                wp_bd=wp_bd, pool_scale=pool_scale[None, :])


def _rows_layout(x4):
    x = jnp.swapaxes(x4, 1, 2)
    pad = [(0, 0)] * x.ndim
    pad[2] = (0, 8 - N_KV_HEADS)
    x = jnp.pad(x, pad)
    return x.reshape((x.shape[0], ROWS) + x.shape[3:])


def _token_minor_to_logical(xt, lead):
    t = xt.shape[-1]
    x = xt.reshape(lead + (-1, N_KV_HEADS, HEAD_DIM, t))
    n = len(lead)
    return jnp.transpose(x, tuple(range(n)) + (n + 3, n, n + 1, n + 2))


def kernel(x_prompt, x_sample, cache_kv, state_kv_win, state_pool, page_table, ln_gain, w_in, q_norm, k_norm, w_cmp1, b_cmp1, w_cmp2, w_pool, pool_scale, w_out):
    assert cache_kv.shape[0] == 1 and x_sample.shape[1] == 1
    b, seq, _ = x_prompt.shape
    nb = x_sample.shape[0]
    n_pages, page_size = page_table.shape[1], cache_kv.shape[2]
    past = n_pages * page_size
    win_buf = state_kv_win.shape[2]
    w = _prep_weights(w_in[0], w_cmp1[0], b_cmp1[0], w_cmp2[0], w_pool[0], pool_scale[0], w_out[0])

    z, sgp, q, sgn, gates, kvt, kvwt = _project(x_prompt, jnp.arange(seq), 256, ln_gain[0], w["w_in"], q_norm[0],
                                                 k_norm[0], emit_rows=False)
    gp = _pool_prompt(z, sgp, w["wp_bd"], w["pool_scale"])
    kc, vc = _compress_prompt(kvt, w["w1cat"], w["b1t"], w["w2bd"])
    o_nsa = _nsa_prompt(q, gates, kc, vc, kvt, kvwt)
    y_p = _out_proj(x_prompt.reshape(b * seq, D_MODEL), gp.reshape(b * seq, POOL_WIDTH),
                    o_nsa.reshape(b * seq, NSA_WIDTH), sgn.reshape(b * seq, NSA_WIDTH), w["w_out"], 512)

    xs = x_sample.reshape(1, nb, D_MODEL)
    zs, sgps, qs, sgns, gates_s, kvst, kvwst, kvs, kvws = [
        a[0] for a in _project(xs, jnp.full((nb,), past), nb, ln_gain[0], w["w_in"], q_norm[0], k_norm[0], emit_rows=True)]
    cache_t = jnp.transpose(cache_kv[0], (0, 2, 3, 4, 1)).reshape(cache_kv.shape[1], 4 * KV_WIDTH, page_size)
    win_t = jnp.transpose(state_kv_win[0], (0, 2, 3, 4, 1)).reshape(nb, 2 * KV_WIDTH, win_buf)
    q4 = qs.reshape(nb, N_KV_HEADS, GQA, HEAD_DIM) * SCALE
    eye = jnp.eye(N_KV_HEADS, dtype=F32)
    qbd = _rows_layout(jnp.einsum("bhgd,hk->bhgkd", q4, eye).reshape(nb, N_KV_HEADS, GQA, KV_WIDTH))
    q4h = jnp.swapaxes(_rows_layout(jnp.einsum("bhgd,hk->bhgkd", q4, eye)), 1, 2)
    ocmp, idx = _compress_decode(page_table, cache_t, qbd, w["w1cat"], w["b1t"], w["w2bd"])
    gate_rows = _rows_layout(gates_s.reshape(nb, N_KV_HEADS, GQA, 3))
    o_rows = _nsa_decode(page_table, idx[:, :N_KV_HEADS, :], cache_t, qbd, q4h, kvs, kvws, win_t, gate_rows, ocmp)
    o_s = jnp.swapaxes(o_rows.reshape(nb, GQA, 8, HEAD_DIM)[:, :, :N_KV_HEADS], 1, 2).reshape(nb, NSA_WIDTH)
    gps = _pool_sample(jnp.swapaxes(state_pool[0], 0, 1), zs, sgps, w["wp_bd"], w["pool_scale"], past)
    y_s = _out_proj(x_sample.reshape(nb, D_MODEL), gps, o_s, sgns, w["w_out"], nb)

    kv_prompt = _token_minor_to_logical(kvt, (b,))[None]
    kv_sample = _token_minor_to_logical(kvst, ())[None, :, None]
    win_keep = min(WINDOW, seq)
    win_prompt = _token_minor_to_logical(kvwt[:, :, seq - win_keep:], (b,))[None]
    win_sample = _token_minor_to_logical(jnp.concatenate([win_t, kvws[:, :, None]], axis=2)[:, :, 1:], (nb,))[None]
    pool_prompt = z[:, seq - POOL_HIST:][None]
    pool_sample = jnp.concatenate([state_pool[0], zs[:, None, :]], axis=1)[:, 1:][None]
    return (y_p.reshape(b, seq, D_MODEL), y_s.reshape(nb, 1, D_MODEL), kv_prompt, kv_sample,
            win_prompt, win_sample, pool_prompt, pool_sample)
```

```python
import functools

import numpy as np
import jax
import jax.numpy as jnp
from jax import lax
from jax.experimental import pallas as pl
from jax.experimental.pallas import tpu as pltpu

F32 = jnp.float32
BF16 = jnp.bfloat16

D_MODEL = 1024
POOL_WIDTH = 256
POOL_WINDOWS = (2, 4, 8, 16)
POOL_GROUP_DIM = 64
POOL_HIST = 15
HEAD_DIM = 64
N_HEADS = 12
N_KV_HEADS = 4
GQA = 3
NSA_WIDTH = 768
KV_WIDTH = 256
CMP_BLOCK = 32
CMP_STRIDE = 16
SEL_BLOCK = 64
SEL_TOP_N = 16
WINDOW = 512
ROPE_DIM = 16
ROPE_THETA = 500000.0
EPS = 1e-6
FORCE_SCORE = 1e4
IN_SPLITS = (256, 256, 768, 768, 1024, 512, 36)
IN_OFFS = tuple(int(v) for v in np.cumsum((0,) + IN_SPLITS))
SCALE = HEAD_DIM ** -0.5

LANES = 128
VMEM_LIMIT_BYTES = 56 * 1024 * 1024

NEG_INF = float("-inf")
N_SLABS = 2 * KV_WIDTH // LANES


def _cparams(semantics):
    return pltpu.CompilerParams(dimension_semantics=semantics, vmem_limit_bytes=VMEM_LIMIT_BYTES)


def _dot(a, b):
    return jnp.dot(a, b, preferred_element_type=F32)


def _dot_nt(a, b):
    return lax.dot_general(a, b, (((1,), (1,)), ((), ())), preferred_element_type=F32)


def _split_dot(x, w_bf16):
    hi = x.astype(BF16)
    lo = (x - hi.astype(F32)).astype(BF16)
    return _dot(hi, w_bf16) + _dot(lo, w_bf16)


def _silu(x):
    return x * jax.nn.sigmoid(x)


def _safe_softmax(s):
    m = jnp.max(s, axis=-1, keepdims=True)
    m = jnp.where(m == NEG_INF, 0.0, m)
    e = jnp.exp(s - m)
    return e / jnp.maximum(jnp.sum(e, axis=-1, keepdims=True), 1e-30)


def _head_norm_rope(v, gain, bd, cos, sin_lo, sin_hi):
    outs = []
    for c in range(v.shape[1] // LANES):
        vc = v[:, c * LANES:(c + 1) * LANES]
        ms = _split_dot(vc * vc, bd)
        n = vc * lax.rsqrt(ms + EPS) * gain[:, c * LANES:(c + 1) * LANES]
        outs.append(n * cos + pltpu.roll(n, LANES - 8, 1) * sin_lo + pltpu.roll(n, 8, 1) * sin_hi)
    return jnp.concatenate(outs, axis=1)


def _head_norm_rope_t(vt, gain_t, cos_t, sin_t):
    outs = []
    for c in range(vt.shape[1] // LANES):
        sl = slice(c * LANES, (c + 1) * LANES)
        v = vt[:, sl].reshape(N_KV_HEADS, HEAD_DIM, LANES)
        ms = jnp.mean(v * v, axis=1, keepdims=True)
        n = v * lax.rsqrt(ms + EPS) * gain_t.reshape(N_KV_HEADS, HEAD_DIM, LANES)
        half = ROPE_DIM // 2
        x1, x2 = n[:, 0:half, :], n[:, half:ROPE_DIM, :]
        cos, sin = cos_t[:, sl][None], sin_t[:, sl][None]
        r = jnp.concatenate([x1 * cos - x2 * sin, x2 * cos + x1 * sin, n[:, ROPE_DIM:, :]], axis=1)
        outs.append(r.reshape(N_KV_HEADS * HEAD_DIM, LANES))
    return jnp.concatenate(outs, axis=1)


def _project_kernel(x_ref, gain_ref, w_ref, wt_ref, qg_ref, kgt_ref, bd_ref, cos_ref, slo_ref, shi_ref, cost_ref, sint_ref,
                    z_ref, sgp_ref, q_ref, sgn_ref, gate_ref, kvt_ref, kvwt_ref, *row_refs):
    x = x_ref[0]
    ms = jnp.mean(x * x, axis=-1, keepdims=True)
    hb = (x * lax.rsqrt(ms + EPS) * gain_ref[...]).astype(BF16)

    def proj(lo, hi):
        return _dot(hb, w_ref[:, lo:hi])

    o = IN_OFFS
    z_ref[0] = proj(o[0], o[1])
    sgp_ref[0] = _silu(proj(o[1], o[2]))
    q_ref[0] = _head_norm_rope(proj(o[2], o[3]), qg_ref[...], bd_ref[...], cos_ref[...], slo_ref[...], shi_ref[...])
    sgn_ref[0] = _silu(proj(o[3], o[4]))
    gate_ref[0] = jax.nn.sigmoid(proj(o[6], o[7]))
    key_gain = {0: 0, 2: 1, 4: 2}
    cos_t, sin_t = cost_ref[...], sint_ref[...]
    for grp in range(6):
        val_t = _dot_nt(wt_ref[grp * KV_WIDTH:(grp + 1) * KV_WIDTH, :], hb)
        if grp in key_gain:
            val_t = _head_norm_rope_t(val_t, kgt_ref[key_gain[grp]], cos_t, sin_t)
        dst, r0 = (kvt_ref, grp * KV_WIDTH) if grp < 4 else (kvwt_ref, (grp - 4) * KV_WIDTH)
        dst[0, r0:r0 + KV_WIDTH, :] = val_t
        if row_refs:
            row_refs[0 if grp < 4 else 1][0, :, r0:r0 + KV_WIDTH] = val_t.T


def _rope_tables(pos):
    half = ROPE_DIM // 2
    freqs = ROPE_THETA ** (-jnp.arange(half, dtype=F32) / half)
    ang = pos.astype(F32)[:, None] * freqs[None, :]
    cos8, sin8 = jnp.cos(ang), jnp.sin(ang)
    p = pos.shape[0]
    rest = HEAD_DIM - ROPE_DIM
    cos = jnp.concatenate([cos8, cos8, jnp.ones((p, rest), F32)], axis=1)
    slo = jnp.concatenate([-sin8, jnp.zeros((p, HEAD_DIM - half), F32)], axis=1)
    shi = jnp.concatenate([jnp.zeros((p, half), F32), sin8, jnp.zeros((p, rest), F32)], axis=1)
    return tuple(jnp.tile(t, (1, LANES // HEAD_DIM)) for t in (cos, slo, shi)) + (cos8.T, sin8.T)


def _project(x, pos, tm, ln_gain, w_bf16, q_norm, k_norm, emit_rows):
    b, s, _ = x.shape
    assert pos.shape[0] == s and tm % LANES == 0
    cos, slo, shi, cos_t, sin_t = _rope_tables(pos)
    tab_spec = pl.BlockSpec((tm, LANES), lambda bb, i: (i, 0))
    tab_t_spec = pl.BlockSpec((ROPE_DIM // 2, tm), lambda bb, i: (0, i))
    bd = jnp.asarray(np.kron(np.eye(LANES // HEAD_DIM), np.full((HEAD_DIM, HEAD_DIM), 1.0 / HEAD_DIM)), BF16)
    qg = jnp.tile(q_norm, N_HEADS)[None, :]
    kgt = jnp.broadcast_to(jnp.tile(k_norm, (1, N_KV_HEADS))[:, :, None], (k_norm.shape[0], KV_WIDTH, LANES))
    gain = ln_gain[None, :]
    wt = w_bf16[:, IN_OFFS[4]:IN_OFFS[6]].T

    def full(a):
        return pl.BlockSpec(a.shape, lambda bb, i: (0,) * a.ndim)

    def rows(w):
        return pl.BlockSpec((1, tm, w), lambda bb, i: (bb, i, 0))

    def cols(f):
        return pl.BlockSpec((1, f, tm), lambda bb, i: (bb, 0, i))

    row_w = (256, 256, 768, 768, 36)
    out_specs = [rows(w) for w in row_w] + [cols(4 * KV_WIDTH), cols(2 * KV_WIDTH)]
    out_shape = ([jax.ShapeDtypeStruct((b, s, w), F32) for w in row_w]
                 + [jax.ShapeDtypeStruct((b, 4 * KV_WIDTH, s), F32), jax.ShapeDtypeStruct((b, 2 * KV_WIDTH, s), F32)])
    if emit_rows:
        out_specs += [rows(4 * KV_WIDTH), rows(2 * KV_WIDTH)]
        out_shape += [jax.ShapeDtypeStruct((b, s, 4 * KV_WIDTH), F32), jax.ShapeDtypeStruct((b, s, 2 * KV_WIDTH), F32)]
    return pl.pallas_call(
        _project_kernel,
        grid=(b, s // tm),
        in_specs=[rows(D_MODEL), full(gain), full(w_bf16), full(wt), full(qg), full(kgt), full(bd),
                  tab_spec, tab_spec, tab_spec, tab_t_spec, tab_t_spec],
        out_specs=out_specs,
        out_shape=out_shape,
        compiler_params=_cparams(("parallel", "parallel")),
        name="project",
    )(x, gain, w_bf16, wt, qg, kgt, bd, cos, slo, shi, cos_t, sin_t)


def _pool_prompt_kernel(z_ref, sgp_ref, wp_ref, scale_ref, out_ref, pad_ref):
    length = z_ref.shape[1]
    pad_ref[0:16, :] = jnp.zeros((16, POOL_WIDTH), F32)
    pad_ref[16:16 + length, :] = z_ref[0]
    z = pad_ref[16:16 + length, :]
    acc = z
    sums = {}
    for k in range(1, POOL_HIST + 1):
        acc = acc + pad_ref[16 - k:16 - k + length, :]
        if k + 1 in POOL_WINDOWS:
            sums[k + 1] = acc
    pos = lax.broadcasted_iota(jnp.int32, (length, 1), 0)
    lane = lax.broadcasted_iota(jnp.int32, (1, POOL_WIDTH), 1)
    means = None
    for gi, w in enumerate(POOL_WINDOWS):
        m = sums[w] / jnp.minimum(w, pos + 1).astype(F32)
        means = m if means is None else jnp.where(lane >= gi * POOL_GROUP_DIM, m, means)
    mixed = _dot((means - z).astype(BF16), wp_ref[...])
    out_ref[0] = mixed * scale_ref[...] * sgp_ref[0]


def _pool_prompt(z, sgp, wp_bd, scale):
    b, length, _ = z.shape
    blk = pl.BlockSpec((1, length, POOL_WIDTH), lambda i: (i, 0, 0))
    return pl.pallas_call(
        _pool_prompt_kernel,
        grid=(b,),
        in_specs=[blk, blk, pl.BlockSpec(wp_bd.shape, lambda i: (0, 0)), pl.BlockSpec(scale.shape, lambda i: (0, 0))],
        out_specs=blk,
        out_shape=jax.ShapeDtypeStruct(z.shape, F32),
        scratch_shapes=[pltpu.VMEM((16 + length, POOL_WIDTH), F32)],
        compiler_params=_cparams(("parallel",)),
        name="pool_prompt",
    )(z, sgp, wp_bd, scale)


def _compress_accumulate(load_rows, w1_ref, slot, nj):
    acc = jnp.zeros((nj, 2 * LANES), F32)
    for s2 in range(CMP_STRIDE // 2):
        lhs = jnp.concatenate([load_rows(2 * s2), load_rows(2 * s2 + 1)], axis=1).astype(BF16)
        acc = acc + _dot(lhs, w1_ref[slot, s2])
    return acc


def _compress_finish(acc, b1, w2):
    nj = acc.shape[0]
    second = pltpu.roll(acc[:, LANES:], nj - 1, 0)
    hid = _silu(acc[:, :LANES] + second + b1)
    return _dot(hid.astype(BF16), w2)


def _compress_prompt_kernel(kvt_ref, w1_ref, b1_ref, w2_ref, kc_ref, vc_ref, xbuf):
    length = kvt_ref.shape[2]
    nj = length // CMP_STRIDE
    for sl in range(N_SLABS):
        for tt in range(length // LANES):
            xbuf[sl, tt * LANES:(tt + 1) * LANES, :] = kvt_ref[0, sl * LANES:(sl + 1) * LANES, tt * LANES:(tt + 1) * LANES].T
    for sl in range(N_SLABS):
        slot, out_ref = sl // 2, (kc_ref, vc_ref)[sl // 2]
        acc = _compress_accumulate(lambda s, sl=sl: xbuf[sl, pl.ds(s, nj, stride=CMP_STRIDE), :], w1_ref, slot, nj)
        out_ref[0, :, (sl % 2) * LANES:(sl % 2 + 1) * LANES] = _compress_finish(acc, b1_ref[slot], w2_ref[slot])


def _compress_prompt(kvt, w1cat, b1t, w2bd):
    b, _, length = kvt.shape
    nj = length // CMP_STRIDE

    def full(a):
        return pl.BlockSpec(a.shape, lambda bb: (0,) * a.ndim)

    out_blk = pl.BlockSpec((1, nj, KV_WIDTH), lambda bb: (bb, 0, 0))
    return pl.pallas_call(
        _compress_prompt_kernel,
        grid=(b,),
        in_specs=[pl.BlockSpec((1, 2 * KV_WIDTH, length), lambda bb: (bb, 0, 0)), full(w1cat), full(b1t), full(w2bd)],
        out_specs=[out_blk, out_blk],
        out_shape=[jax.ShapeDtypeStruct((b, nj, KV_WIDTH), F32)] * 2,
        scratch_shapes=[pltpu.VMEM((N_SLABS, length, LANES), F32)],
        compiler_params=_cparams(("parallel",)),
        name="compress_prompt",
    )(kvt, w1cat, b1t, w2bd)


def _compress_weights(w_cmp1, b_cmp1, w_cmp2):
    eye2 = jnp.eye(LANES // HEAD_DIM, dtype=F32)

    def bd2(m):
        return jnp.kron(eye2, m)

    w1cat = []
    for slot in range(2):
        per = []
        for s2 in range(CMP_STRIDE // 2):
            blocks = []
            for s in (2 * s2, 2 * s2 + 1):
                blocks.append(jnp.concatenate([bd2(w_cmp1[slot, s]), bd2(w_cmp1[slot, CMP_STRIDE + s])], axis=1))
            per.append(jnp.concatenate(blocks, axis=0))
        w1cat.append(jnp.stack(per))
    w1cat = jnp.stack(w1cat).astype(BF16)
    b1t = jnp.tile(b_cmp1, (1, LANES // HEAD_DIM))[:, None, :]
    w2bd = jnp.stack([bd2(w_cmp2[0]), bd2(w_cmp2[1])]).astype(BF16)
    return w1cat, b1t, w2bd


def _overlap_matrix(n_cmp_pad, n_cmp, n_sel, n_sel_pad):
    cstart = np.arange(n_cmp_pad)[:, None] * CMP_STRIDE
    bstart = np.arange(n_sel_pad)[None, :] * SEL_BLOCK
    ov = np.clip(np.minimum(cstart + CMP_BLOCK, bstart + SEL_BLOCK) - np.maximum(cstart, bstart), 0, None) / CMP_BLOCK
    ov = ov * (np.arange(n_cmp_pad)[:, None] < n_cmp) * (np.arange(n_sel_pad)[None, :] < n_sel)
    return jnp.asarray(ov, BF16)


def _topk_mask(score, n_iter):
    lane = lax.broadcasted_iota(jnp.int32, score.shape, 1)
    rank = jnp.zeros(score.shape, F32)
    for j in range(n_iter):
        col = score[:, j:j + 1]
        ahead = (col > score) | ((col == score) & (lane > j))
        rank = rank + jnp.where(ahead, 1.0, 0.0)
    return rank, (rank < SEL_TOP_N) & (score > NEG_INF)


TQ = 128
WIN_TILES = WINDOW // TQ + 1
Q_GROUPS = 4


def _topk_mask_t(score_t):
    row = lax.broadcasted_iota(jnp.int32, score_t.shape, 0)
    rank = jnp.zeros(score_t.shape, F32)
    for j in range(score_t.shape[0]):
        cur = score_t[j:j + 1, :]
        ahead = (cur > score_t) | ((cur == score_t) & (row > j))
        rank = rank + jnp.where(ahead, 1.0, 0.0)
    return (rank < SEL_TOP_N) & (score_t > NEG_INF)


def _attend_post(s, v_t, half):
    m = jnp.max(s, axis=-1, keepdims=True)
    m = jnp.where(m == NEG_INF, 0.0, m)
    e = jnp.exp(s - m)
    row_half = lax.broadcasted_iota(jnp.int32, (v_t.shape[0], 1), 0) // HEAD_DIM
    out = _dot_nt(e.astype(BF16), jnp.where(row_half == half, v_t, jnp.ones_like(v_t)))
    return out / jnp.maximum(pltpu.roll(out, HEAD_DIM, 1), 1e-30)


def _nsa_prompt_kernel(*refs, tile0):
    q_ref, gate_ref, kc_ref, vc_ref, kst_ref, vst_ref = refs[:6]
    win_refs, (ovt_ref, ex_ref, out_ref) = refs[6:6 + WIN_TILES], refs[6 + WIN_TILES:]
    i = tile0 + pl.program_id(1)
    nk = kst_ref.shape[2]
    n_cmp = kc_ref.shape[1]
    n_sel = ex_ref.shape[0]
    qpos = i * TQ + lax.broadcasted_iota(jnp.int32, (TQ, 1), 0)
    qpos_l = i * TQ + lax.broadcasted_iota(jnp.int32, (1, TQ), 1)
    lane = lax.broadcasted_iota(jnp.int32, (1, LANES), 1)
    lane_half = lane // HEAD_DIM

    ncol = lax.broadcasted_iota(jnp.int32, (1, n_cmp), 1)
    bias_c = jnp.where(ncol * CMP_STRIDE + CMP_BLOCK - 1 <= qpos, 0.0, NEG_INF)
    bias_w = []
    for u in range(WIN_TILES):
        kb = i - (WIN_TILES - 1) + u
        rel = qpos - (kb * TQ + lane)
        bias_w.append(jnp.where((kb >= 0) & (rel >= 0) & (rel < WINDOW), 0.0, NEG_INF))
    bias_w = jnp.concatenate(bias_w, axis=1)
    jrow = lax.broadcasted_iota(jnp.int32, (n_sel, 1), 0)
    qblk_l = qpos_l // SEL_BLOCK
    valid_t = jrow * SEL_BLOCK <= qpos_l
    forced_t = (jrow == 0) | (jrow == qblk_l) | (jrow == qblk_l - 1)
    causal = lax.broadcasted_iota(jnp.int32, (1, nk), 1) <= qpos
    gates = gate_ref[0]

    def add_bias(s, bias):
        return (s.reshape(GQA, TQ, s.shape[-1]) + bias[None]).reshape(GQA * TQ, s.shape[-1])

    heads = range(N_KV_HEADS)
    span = [((h // 2) * LANES, (h // 2 + 1) * LANES) for h in heads]

    qzs = []
    for h in heads:
        rows = []
        for g in range(GQA):
            qh = h * GQA + g
            x = q_ref[0, :, (qh // 2) * LANES:(qh // 2 + 1) * LANES] * SCALE
            if qh % 2 != h % 2:
                x = pltpu.roll(x, HEAD_DIM, 1)
            rows.append(jnp.where(lane_half == h % 2, x, 0.0))
        qzs.append(jnp.concatenate(rows, axis=0).astype(BF16))

    o_cmps, score_ts = [], []
    for h in heads:
        lo, hi = span[h]
        p_c = _safe_softmax(add_bias(_dot_nt(qzs[h], kc_ref[0, :, lo:hi].astype(BF16)), bias_c))
        o_cmps.append(_dot(p_c.astype(BF16), vc_ref[0, :, lo:hi].astype(BF16)))
        p_sum = p_c[0:TQ] + p_c[TQ:2 * TQ] + p_c[2 * TQ:3 * TQ]
        p_hi = p_sum.astype(BF16)
        p_lo = (p_sum - p_hi.astype(F32)).astype(BF16)
        imp_t = _dot_nt(ovt_ref[...], p_hi) + _dot_nt(ovt_ref[...], p_lo)
        score_ts.append(jnp.where(valid_t, jnp.where(forced_t, FORCE_SCORE, imp_t), NEG_INF))

    sels, o_wins = [], []
    for h in heads:
        lo, hi = span[h]
        sels.append(jnp.where(_topk_mask_t(score_ts[h]), 1.0, 0.0).T.astype(BF16))
        kwt = jnp.concatenate([w[0, lo:hi, :] for w in win_refs], axis=1).astype(BF16)
        vwt = jnp.concatenate([w[0, KV_WIDTH + lo:KV_WIDTH + hi, :] for w in win_refs], axis=1).astype(BF16)
        o_wins.append(_attend_post(add_bias(_dot(qzs[h], kwt), bias_w), vwt, h % 2))

    o_slcs = []
    for h in heads:
        lo, hi = span[h]
        selw = _dot(sels[h], ex_ref[...])
        bias_s = jnp.where((selw > 0.5) & causal, 0.0, NEG_INF)
        o_slcs.append(_attend_post(add_bias(_dot(qzs[h], kst_ref[0, lo:hi, :].astype(BF16)), bias_s),
                                   vst_ref[0, lo:hi, :].astype(BF16), h % 2))

    out_chunks = [jnp.zeros((TQ, LANES), F32) for _ in range(NSA_WIDTH // LANES)]
    for h in heads:
        half, o_cmp, o_slc, o_win = h % 2, o_cmps[h], o_slcs[h], o_wins[h]
        for g in range(GQA):
            qh = h * GQA + g
            gi = qh * 3
            sl = slice(g * TQ, (g + 1) * TQ)
            o = (gates[:, gi:gi + 1] * o_cmp[sl] + gates[:, gi + 1:gi + 2] * o_slc[sl] + gates[:, gi + 2:gi + 3] * o_win[sl])
            if qh % 2 != half:
                o = pltpu.roll(o, HEAD_DIM, 1)
            out_chunks[qh // 2] = out_chunks[qh // 2] + jnp.where(lane_half == qh % 2, o, 0.0)
    for c, val in enumerate(out_chunks):
        out_ref[0, :, c * LANES:(c + 1) * LANES] = val


def _nsa_prompt(q, gates, kc, vc, kvt, kvwt):
    b, seq, _ = q.shape
    n_cmp = kc.shape[1]
    n_sel = seq // SEL_BLOCK
    n_tiles = seq // TQ
    assert n_sel % 8 == 0 and n_tiles % Q_GROUPS == 0 and (seq // Q_GROUPS) % LANES == 0
    ovt = _overlap_matrix(n_cmp, (seq - CMP_BLOCK) // CMP_STRIDE + 1, n_sel, n_sel).T
    ex = np.arange(n_sel)[:, None] == (np.arange(seq)[None, :] // SEL_BLOCK)
    tiles_per_group = n_tiles // Q_GROUPS

    outs = []
    for grp in range(Q_GROUPS):
        tile0 = grp * tiles_per_group
        nk = (grp + 1) * (seq // Q_GROUPS)
        ex_g = jnp.asarray(ex[:, :nk], BF16)

        def tile(w, tile0=tile0):
            return pl.BlockSpec((1, TQ, w), lambda bb, i: (bb, tile0 + i, 0))

        def whole(rows):
            return pl.BlockSpec((1, rows, KV_WIDTH), lambda bb, i: (bb, 0, 0))

        def feat_rows(row_blk, nk=nk):
            return pl.BlockSpec((1, KV_WIDTH, nk), lambda bb, i, rb=row_blk: (bb, rb, 0))

        def win_tile(u, tile0=tile0):
            return pl.BlockSpec((1, 2 * KV_WIDTH, TQ),
                                lambda bb, i, u=u: (bb, 0, jnp.maximum(tile0 + i - (WIN_TILES - 1) + u, 0)))

        def full(a):
            return pl.BlockSpec(a.shape, lambda bb, i: (0,) * a.ndim)

        in_specs = ([tile(NSA_WIDTH), tile(gates.shape[-1]), whole(n_cmp), whole(n_cmp), feat_rows(2), feat_rows(3)]
                    + [win_tile(u) for u in range(WIN_TILES)] + [full(ovt), full(ex_g)])
        args = [q, gates, kc, vc, kvt, kvt] + [kvwt] * WIN_TILES + [ovt, ex_g]
        outs.append(pl.pallas_call(
            functools.partial(_nsa_prompt_kernel, tile0=tile0),
            grid=(b, tiles_per_group),
            in_specs=in_specs,
            out_specs=pl.BlockSpec((1, TQ, NSA_WIDTH), lambda bb, i: (bb, i, 0)),
            out_shape=jax.ShapeDtypeStruct((b, seq // Q_GROUPS, NSA_WIDTH), F32),
            compiler_params=_cparams(("parallel", "arbitrary")),
            name=f"nsa_prompt_g{grp}",
        )(*args))
    return outs


def _out_proj_kernel(x_ref, gp_ref, o_ref, sgn_ref, w_ref, y_ref):
    gn = (o_ref[...] * sgn_ref[...]).astype(BF16)
    y_ref[...] = (x_ref[...] + _dot(gp_ref[...].astype(BF16), w_ref[0:POOL_WIDTH, :])
                  + _dot(gn, w_ref[POOL_WIDTH:, :]))


def _out_proj_groups_kernel(x_ref, gp_ref, sgn_ref, w_ref, *rest):
    o_refs, y_ref = rest[:-1], rest[-1]
    g = pl.program_id(1)
    o = o_refs[0][0]
    for k in range(1, len(o_refs)):
        o = jnp.where(g == k, o_refs[k][0], o)
    gn = (o * sgn_ref[0]).astype(BF16)
    y_ref[0] = (x_ref[0] + _dot(gp_ref[0].astype(BF16), w_ref[0:POOL_WIDTH, :]) + _dot(gn, w_ref[POOL_WIDTH:, :]))


def _out_proj_groups(x, gp, o_groups, sgn, w_bf16):
    b, seq, _ = x.shape
    n_groups = len(o_groups)
    tm = seq // n_groups

    def rows(w):
        return pl.BlockSpec((1, tm, w), lambda bb, g: (bb, g, 0))

    grp = pl.BlockSpec((1, tm, NSA_WIDTH), lambda bb, g: (bb, 0, 0))
    return pl.pallas_call(
        _out_proj_groups_kernel,
        grid=(b, n_groups),
        in_specs=[rows(D_MODEL), rows(POOL_WIDTH), rows(NSA_WIDTH), pl.BlockSpec(w_bf16.shape, lambda bb, g: (0, 0))]
        + [grp] * n_groups,
        out_specs=rows(D_MODEL),
        out_shape=jax.ShapeDtypeStruct((b, seq, D_MODEL), F32),
        compiler_params=_cparams(("parallel", "arbitrary")),
        name="out_proj_prompt",
    )(x, gp, sgn, w_bf16, *o_groups)


def _out_proj(x, gp, o_nsa, sgn, w_bf16, tm):
    t = x.shape[0]

    def rows(w):
        return pl.BlockSpec((tm, w), lambda i: (i, 0))

    return pl.pallas_call(
        _out_proj_kernel,
        grid=(t // tm,),
        in_specs=[rows(D_MODEL), rows(POOL_WIDTH), rows(NSA_WIDTH), rows(NSA_WIDTH),
                  pl.BlockSpec(w_bf16.shape, lambda i: (0, 0))],
        out_specs=rows(D_MODEL),
        out_shape=jax.ShapeDtypeStruct((t, D_MODEL), F32),
        compiler_params=_cparams(("parallel",)),
        name="out_proj",
    )(x, gp, o_nsa, sgn, w_bf16)


ROWS = GQA * 8
CH_PAGES = 32


def _page_copy(cache_ref, xt, sem, page, p, slot):
    return pltpu.make_async_copy(cache_ref.at[page, pl.ds(0, 2 * KV_WIDTH), :], xt.at[slot, p], sem.at[slot])


def _compress_decode_kernel(pt_ref, cache_ref, qbd_ref, perm_ref, w1_ref, b1_ref, w2_ref, ov_ref, ocmp_ref, idx_ref,
                            xt, xs, fg, sem, *, n_pages, page_size, n_chunks, nch):
    t = pl.program_id(0)
    bpp = page_size // CMP_STRIDE
    jch = CH_PAGES * bpp
    nj = nch * jch
    past = n_pages * page_size

    def chunk_dma(step, slot, wait):
        base = (step // nch) * n_pages + (step % nch) * CH_PAGES

        def body(p, carry):
            cp = _page_copy(cache_ref, xt, sem, pt_ref[base + p], p, slot)
            cp.wait() if wait else cp.start()
            return carry
        lax.fori_loop(0, CH_PAGES, body, 0, unroll=8)

    @pl.when(t == 0)
    def _():
        chunk_dma(0, 0, wait=False)
        xs[...] = jnp.zeros(xs.shape, BF16)

    @pl.when(t + 1 < n_chunks)
    def _():
        chunk_dma(t + 1, (t + 1) % 2, wait=False)

    slot = t % 2

    @pl.when(t < n_chunks)
    def _():
        chunk_dma(t, slot, wait=True)

    for pp in range(CH_PAGES // 2):
        ys = [_dot_nt(perm_ref[...], xt[slot, 2 * pp + k].astype(BF16)) for k in range(2)]
        for s in range(CMP_STRIDE):
            rows = jnp.concatenate([y[s * bpp:(s + 1) * bpp, :] for y in ys], axis=0)
            xs[slot, s, pp * 2 * bpp:(pp + 1) * 2 * bpp, :] = rows.astype(BF16)

    prev = 1 - slot
    row0 = pl.multiple_of(((t + nch - 1) % nch) * jch, jch)
    for sl in range(N_SLABS):
        acc = jnp.zeros((jch, 2 * LANES), F32)
        for s2 in range(CMP_STRIDE // 2):
            lhs = jnp.concatenate([xs[prev, 2 * s2, :, sl * LANES:(sl + 1) * LANES],
                                   xs[prev, 2 * s2 + 1, :, sl * LANES:(sl + 1) * LANES]], axis=1)
            acc = acc + _dot(lhs, w1_ref[sl // 2, s2])
        fg[sl, pl.ds(row0, jch), :] = acc

    @pl.when((t > 0) & (t % nch == 0))
    def _():
        kcv = [_compress_finish(fg[sl], b1_ref[sl // 2], w2_ref[sl // 2]) for sl in range(N_SLABS)]
        kc = jnp.concatenate(kcv[0:2], axis=1).astype(BF16)
        vc = jnp.concatenate(kcv[2:4], axis=1).astype(BF16)
        s = _dot_nt(qbd_ref[0].astype(BF16), kc)
        ncol = lax.broadcasted_iota(jnp.int32, (1, nj), 1)
        s = s + jnp.where(ncol * CMP_STRIDE + CMP_BLOCK - 1 <= past, 0.0, NEG_INF)
        p = _safe_softmax(s)
        ocmp_ref[0] = _dot(p.astype(BF16), vc)
        p_sum = p[0:8] + p[8:16] + p[16:24]
        imp = _split_dot(p_sum, ov_ref[...])
        lane = lax.broadcasted_iota(jnp.int32, (1, imp.shape[1]), 1)
        qblk = past // SEL_BLOCK
        forced = (lane == 0) | (lane == qblk) | (lane == qblk - 1)
        score = jnp.where(lane * SEL_BLOCK <= past, jnp.where(forced, FORCE_SCORE, imp), NEG_INF)
        rank, sel = _topk_mask(score, qblk + 1)
        lanef = lane.astype(F32)
        lane16 = lax.broadcasted_iota(jnp.int32, (1, SEL_TOP_N), 1)
        idx = jnp.zeros((8, SEL_TOP_N), F32)
        for r in range(SEL_TOP_N):
            v = jnp.sum(jnp.where((rank == r) & sel, lanef, 0.0), axis=-1, keepdims=True)
            idx = jnp.where(lane16 == r, v, idx)
        idx_ref[0] = idx.astype(jnp.int32)


def _compress_decode(page_table, cache_t, qbd, w1cat, b1t, w2bd):
    nb, n_pages = page_table.shape
    page_size = cache_t.shape[2]
    assert n_pages % CH_PAGES == 0 and page_size % LANES == 0
    nch = n_pages // CH_PAGES
    past = n_pages * page_size
    nj = past // CMP_STRIDE
    n_sel = past // SEL_BLOCK + 1
    assert n_sel > SEL_TOP_N
    n_sel_pad = -(-n_sel // LANES) * LANES
    ov = _overlap_matrix(nj, (past + 1 - CMP_BLOCK) // CMP_STRIDE + 1, n_sel, n_sel_pad)
    bpp = page_size // CMP_STRIDE
    tok = np.arange(page_size)
    perm = jnp.asarray(((tok % CMP_STRIDE) * bpp + tok // CMP_STRIDE)[None, :] == np.arange(page_size)[:, None], BF16)

    def full(a):
        return pl.BlockSpec(a.shape, lambda t, pt: (0,) * a.ndim)

    def per_b(shape):
        return pl.BlockSpec((1,) + shape, lambda t, pt: (jnp.maximum(t - 1, 0) // nch, 0, 0))

    n_chunks = nb * nch
    grid_spec = pltpu.PrefetchScalarGridSpec(
        num_scalar_prefetch=1,
        grid=(n_chunks + 1,),
        in_specs=[pl.BlockSpec(memory_space=pl.ANY), per_b((ROWS, KV_WIDTH)), full(perm), full(w1cat), full(b1t),
                  full(w2bd), full(ov)],
        out_specs=[per_b((ROWS, KV_WIDTH)), per_b((8, SEL_TOP_N))],
        scratch_shapes=[pltpu.VMEM((2, CH_PAGES, 2 * KV_WIDTH, page_size), F32),
                        pltpu.VMEM((2, CMP_STRIDE, CH_PAGES * bpp, 2 * KV_WIDTH), BF16),
                        pltpu.VMEM((N_SLABS, nj, 2 * LANES), F32),
                        pltpu.SemaphoreType.DMA((2,))],
    )
    return pl.pallas_call(
        functools.partial(_compress_decode_kernel, n_pages=n_pages, page_size=page_size, n_chunks=n_chunks, nch=nch),
        grid_spec=grid_spec,
        out_shape=[jax.ShapeDtypeStruct((nb, ROWS, KV_WIDTH), F32), jax.ShapeDtypeStruct((nb, 8, SEL_TOP_N), jnp.int32)],
        compiler_params=_cparams(("arbitrary",)),
        name="compress_decode",
    )(page_table.reshape(-1), cache_t, qbd, perm, w1cat, b1t, w2bd, ov)


def _block_copy(cache_ref, kvbuf, sem, pt_ref, idx_ref, bb, h, r, slot, n_pages, page_size):
    bpp = page_size // SEL_BLOCK
    j = idx_ref[(bb * N_KV_HEADS + h) * SEL_TOP_N + r]
    jj = jnp.minimum(j, n_pages * bpp - 1)
    page = pt_ref[bb * n_pages + jj // bpp]
    dst0 = pl.multiple_of(r * page_size, page_size)
    return pltpu.make_async_copy(cache_ref.at[page, pl.ds(2, 2), h],
                                 kvbuf.at[slot, h, :, :, pl.ds(dst0, page_size)], sem.at[slot])


def _heads_to_lanes64(x, rowh):
    out = jnp.zeros((x.shape[0], HEAD_DIM), F32)
    for h in range(N_KV_HEADS):
        out = jnp.where(rowh == h, x[:, h * HEAD_DIM:(h + 1) * HEAD_DIM], out)
    return out


def _nsa_decode_kernel(pt_ref, idx_ref, cache_ref, qbd_ref, q4_ref, kvn_ref, kvwn_ref, kvwnt_ref, wint_ref, gate_ref,
                       ocmp_ref, out_ref, winout_ref, kvbuf, sem, *, n_pages, page_size):
    b = pl.program_id(0)
    nb = pl.num_programs(0)
    bpp = page_size // SEL_BLOCK
    n_past_blocks = n_pages * bpp
    nk = SEL_TOP_N * page_size
    win_buf = wint_ref.shape[2]

    def gather(bb, slot, wait):
        for h in range(N_KV_HEADS):
            def body(r, carry, h=h):
                cp = _block_copy(cache_ref, kvbuf, sem, pt_ref, idx_ref, bb, h, r, slot, n_pages, page_size)
                cp.wait() if wait else cp.start()
                return carry
            lax.fori_loop(0, SEL_TOP_N, body, 0)

    @pl.when(b == 0)
    def _():
        gather(0, 0, wait=False)

    nxt, nslot = (b + 1) % nb, (b + 1) % 2
    slot = b % 2
    gather(b, slot, wait=True)

    qbd = qbd_ref[0]
    rowh = lax.broadcasted_iota(jnp.int32, (ROWS, 1), 0) % 8
    kvn = kvn_ref[0]
    k_new, v_new = kvn[:, 2 * KV_WIDTH:3 * KV_WIDTH], kvn[:, 3 * KV_WIDTH:]
    col = lax.broadcasted_iota(jnp.int32, (1, nk), 1)
    col_r, col_blk = col // page_size, (col % page_size) // SEL_BLOCK

    s_new = jnp.sum(qbd * k_new, axis=-1, keepdims=True)
    o_slc = jnp.zeros((ROWS, HEAD_DIM), F32)
    for h in range(N_KV_HEADS):
        bias = jnp.full((1, nk), NEG_INF, F32)
        new_sel = jnp.zeros((1, 1), jnp.int32)
        for r in range(SEL_TOP_N):
            j = idx_ref[(b * N_KV_HEADS + h) * SEL_TOP_N + r]
            in_cache = j < n_past_blocks
            hit = (col_r == r) & (col_blk == j % bpp) & in_cache
            bias = jnp.where(hit, 0.0, bias)
            new_sel = jnp.maximum(new_sel, jnp.where(in_cache, 0, 1))
        s = _dot(q4_ref[0, h].astype(BF16), kvbuf[slot, h, 0].astype(BF16)) + bias
        sn = s_new + jnp.where(new_sel > 0, 0.0, NEG_INF)
        m = jnp.maximum(jnp.max(s, axis=-1, keepdims=True), sn)
        m = jnp.where(m == NEG_INF, 0.0, m)
        e, en = jnp.exp(s - m), jnp.exp(sn - m)
        den = jnp.maximum(jnp.sum(e, axis=-1, keepdims=True) + en, 1e-30)
        pv = _dot_nt(e.astype(BF16), kvbuf[slot, h, 1].astype(BF16))
        o_h = (pv + en * v_new[:, h * HEAD_DIM:(h + 1) * HEAD_DIM]) / den
        o_slc = jnp.where(rowh == h, o_h, o_slc)
        for r in range(SEL_TOP_N):
            _block_copy(cache_ref, kvbuf, sem, pt_ref, idx_ref, nxt, h, r, nslot, n_pages, page_size).start()

    kwn = kvwn_ref[0]
    s = _dot(qbd.astype(BF16), wint_ref[0, 0:KV_WIDTH, :].astype(BF16))
    rel = win_buf - lax.broadcasted_iota(jnp.int32, (1, win_buf), 1)
    s = s + jnp.where((rel >= 0) & (rel < WINDOW), 0.0, NEG_INF)
    sn = jnp.sum(qbd * kwn[:, :KV_WIDTH], axis=-1, keepdims=True)
    m = jnp.maximum(jnp.max(s, axis=-1, keepdims=True), sn)
    e, en = jnp.exp(s - m), jnp.exp(sn - m)
    den = jnp.sum(e, axis=-1, keepdims=True) + en
    o_w = (_dot_nt(e.astype(BF16), wint_ref[0, KV_WIDTH:, :].astype(BF16)) + en * kwn[:, KV_WIDTH:]) / den

    gates = gate_ref[0]
    out_ref[0] = (gates[:, 0:1] * _heads_to_lanes64(ocmp_ref[0], rowh) + gates[:, 1:2] * o_slc
                  + gates[:, 2:3] * _heads_to_lanes64(o_w, rowh))

    seq_lane = lax.broadcasted_iota(jnp.int32, (1, kvwnt_ref.shape[1]), 1)
    new_col = jnp.sum(jnp.where(seq_lane == b, kvwnt_ref[...], 0.0), axis=-1, keepdims=True)
    row_lane = lax.broadcasted_iota(jnp.int32, (1, win_buf), 1)
    winout_ref[0] = jnp.where(row_lane == win_buf - 1, new_col, pltpu.roll(wint_ref[0], win_buf - 1, 1))

    @pl.when(b == nb - 1)
    def _():
        gather(nxt, nslot, wait=True)


def _nsa_decode(page_table, idx, cache_t, qbd, q4, kvs, kvws, kvws_t, win_t, gate_rows, ocmp):
    nb, n_pages = page_table.shape
    page_size = cache_t.shape[2]
    assert page_size % SEL_BLOCK == 0 and page_size % LANES == 0
    cache5 = cache_t.reshape(cache_t.shape[0], 4, N_KV_HEADS, HEAD_DIM, page_size)

    def per_b(shape):
        return pl.BlockSpec((1,) + shape, lambda bb, pt, ix: (bb,) + (0,) * len(shape))

    nk = SEL_TOP_N * page_size
    grid_spec = pltpu.PrefetchScalarGridSpec(
        num_scalar_prefetch=2,
        grid=(nb,),
        in_specs=[pl.BlockSpec(memory_space=pl.ANY), per_b((ROWS, KV_WIDTH)), per_b((N_KV_HEADS, ROWS, HEAD_DIM)),
                  per_b((1, 4 * KV_WIDTH)), per_b((1, 2 * KV_WIDTH)),
                  pl.BlockSpec(kvws_t.shape, lambda bb, pt, ix: (0, 0)), per_b(win_t.shape[1:]), per_b((ROWS, 3)),
                  per_b((ROWS, KV_WIDTH))],
        out_specs=[per_b((ROWS, HEAD_DIM)), per_b(win_t.shape[1:])],
        scratch_shapes=[pltpu.VMEM((2, N_KV_HEADS, 2, HEAD_DIM, nk), F32),
                        pltpu.SemaphoreType.DMA((2,))],
    )
    return pl.pallas_call(
        functools.partial(_nsa_decode_kernel, n_pages=n_pages, page_size=page_size),
        grid_spec=grid_spec,
        out_shape=[jax.ShapeDtypeStruct((nb, ROWS, HEAD_DIM), F32), jax.ShapeDtypeStruct(win_t.shape, F32)],
        compiler_params=_cparams(("arbitrary",)),
        name="nsa_decode",
    )(page_table.reshape(-1), idx.reshape(-1), cache5, qbd, q4, kvs[:, None, :], kvws[:, None, :], kvws_t, win_t,
      gate_rows, ocmp)


def _pool_sample_kernel(hist_ref, z_ref, sgp_ref, wp_ref, scale_ref, out_ref, *, pos):
    z = z_ref[...]
    acc = z
    sums = {}
    for k in range(1, POOL_HIST + 1):
        acc = acc + hist_ref[POOL_HIST - k]
        if k + 1 in POOL_WINDOWS:
            sums[k + 1] = acc
    lane = lax.broadcasted_iota(jnp.int32, (1, POOL_WIDTH), 1)
    means = None
    for gi, w in enumerate(POOL_WINDOWS):
        m = sums[w] / float(min(w, pos + 1))
        means = m if means is None else jnp.where(lane >= gi * POOL_GROUP_DIM, m, means)
    mixed = _dot((means - z).astype(BF16), wp_ref[...])
    out_ref[...] = mixed * scale_ref[...] * sgp_ref[...]


def _pool_sample(hist_t, z, sgp, wp_bd, scale, pos):
    return pl.pallas_call(
        functools.partial(_pool_sample_kernel, pos=pos),
        out_shape=jax.ShapeDtypeStruct(z.shape, F32),
        name="pool_sample",
    )(hist_t, z, sgp, wp_bd, scale)


def _prep_weights(w_in, w_cmp1, b_cmp1, w_cmp2, w_pool, pool_scale, w_out):
    w1cat, b1t, w2bd = _compress_weights(w_cmp1, b_cmp1, w_cmp2)
    wp_bd = jax.scipy.linalg.block_diag(*[w_pool[g] for g in range(len(POOL_WINDOWS))]).astype(BF16)
    return dict(w_in=w_in.astype(BF16), w_out=w_out.astype(BF16), w1cat=w1cat, b1t=b1t, w2bd=w2bd,
                wp_bd=wp_bd, pool_scale=pool_scale[None, :])


def _rows_layout(x4):
    x = jnp.swapaxes(x4, 1, 2)
    pad = [(0, 0)] * x.ndim
    pad[2] = (0, 8 - N_KV_HEADS)
    x = jnp.pad(x, pad)
    return x.reshape((x.shape[0], ROWS) + x.shape[3:])


def _token_minor_to_logical(xt, lead):
    t = xt.shape[-1]
    x = xt.reshape(lead + (-1, N_KV_HEADS, HEAD_DIM, t))
    n = len(lead)
    return jnp.transpose(x, tuple(range(n)) + (n + 3, n, n + 1, n + 2))


def kernel(x_prompt, x_sample, cache_kv, state_kv_win, state_pool, page_table, ln_gain, w_in, q_norm, k_norm, w_cmp1, b_cmp1, w_cmp2, w_pool, pool_scale, w_out):
    assert cache_kv.shape[0] == 1 and x_sample.shape[1] == 1
    b, seq, _ = x_prompt.shape
    nb = x_sample.shape[0]
    n_pages, page_size = page_table.shape[1], cache_kv.shape[2]
    past = n_pages * page_size
    win_buf = state_kv_win.shape[2]
    w = _prep_weights(w_in[0], w_cmp1[0], b_cmp1[0], w_cmp2[0], w_pool[0], pool_scale[0], w_out[0])

    z, sgp, q, sgn, gates, kvt, kvwt = _project(x_prompt, jnp.arange(seq), 512, ln_gain[0], w["w_in"], q_norm[0],
                                                 k_norm[0], emit_rows=False)
    gp = _pool_prompt(z, sgp, w["wp_bd"], w["pool_scale"])
    kc, vc = _compress_prompt(kvt, w["w1cat"], w["b1t"], w["w2bd"])
    o_groups = _nsa_prompt(q, gates, kc, vc, kvt, kvwt)
    y_p = _out_proj_groups(x_prompt, gp, o_groups, sgn, w["w_out"])

    xs = x_sample.reshape(1, nb, D_MODEL)
    zs, sgps, qs, sgns, gates_s, kvst, kvwst, kvs, kvws = [
        a[0] for a in _project(xs, jnp.full((nb,), past), nb, ln_gain[0], w["w_in"], q_norm[0], k_norm[0], emit_rows=True)]
    cache_t = jnp.transpose(cache_kv[0], (0, 2, 3, 4, 1)).reshape(cache_kv.shape[1], 4 * KV_WIDTH, page_size)
    win_t = jnp.transpose(state_kv_win[0], (0, 2, 3, 4, 1)).reshape(nb, 2 * KV_WIDTH, win_buf)
    q4 = qs.reshape(nb, N_KV_HEADS, GQA, HEAD_DIM) * SCALE
    eye = jnp.eye(N_KV_HEADS, dtype=F32)
    qbd = _rows_layout(jnp.einsum("bhgd,hk->bhgkd", q4, eye).reshape(nb, N_KV_HEADS, GQA, KV_WIDTH))
    q4h = jnp.swapaxes(_rows_layout(jnp.einsum("bhgd,hk->bhgkd", q4, eye)), 1, 2)
    ocmp, idx = _compress_decode(page_table, cache_t, qbd, w["w1cat"], w["b1t"], w["w2bd"])
    gate_rows = _rows_layout(gates_s.reshape(nb, N_KV_HEADS, GQA, 3))
    o_rows, win_new_t = _nsa_decode(page_table, idx[:, :N_KV_HEADS, :], cache_t, qbd, q4h, kvs, kvws, kvwst, win_t,
                                    gate_rows, ocmp)
    o_s = jnp.swapaxes(o_rows.reshape(nb, GQA, 8, HEAD_DIM)[:, :, :N_KV_HEADS], 1, 2).reshape(nb, NSA_WIDTH)
    gps = _pool_sample(jnp.swapaxes(state_pool[0], 0, 1), zs, sgps, w["wp_bd"], w["pool_scale"], past)
    y_s = _out_proj(x_sample.reshape(nb, D_MODEL), gps, o_s, sgns, w["w_out"], nb)

    kv_prompt = _token_minor_to_logical(kvt, (b,))[None]
    kv_sample = _token_minor_to_logical(kvst, ())[None, :, None]
    win_keep = min(WINDOW, seq)
    win_prompt = _token_minor_to_logical(kvwt[:, :, seq - win_keep:], (b,))[None]
    win_sample = _token_minor_to_logical(win_new_t, (nb,))[None]
    pool_prompt = z[:, seq - POOL_HIST:][None]
    pool_sample = jnp.concatenate([state_pool[0], zs[:, None, :]], axis=1)[:, 1:][None]
    return (y_p, y_s.reshape(nb, 1, D_MODEL), kv_prompt, kv_sample,
            win_prompt, win_sample, pool_prompt, pool_sample)
```

```python
import functools

import numpy as np
import jax
import jax.numpy as jnp
from jax import lax
from jax.experimental import pallas as pl
from jax.experimental.pallas import tpu as pltpu

F32 = jnp.float32
BF16 = jnp.bfloat16

D_MODEL = 1024
POOL_WIDTH = 256
POOL_WINDOWS = (2, 4, 8, 16)
POOL_GROUP_DIM = 64
POOL_HIST = 15
HEAD_DIM = 64
N_HEADS = 12
N_KV_HEADS = 4
GQA = 3
NSA_WIDTH = 768
KV_WIDTH = 256
CMP_BLOCK = 32
CMP_STRIDE = 16
SEL_BLOCK = 64
SEL_TOP_N = 16
WINDOW = 512
ROPE_DIM = 16
ROPE_THETA = 500000.0
EPS = 1e-6
FORCE_SCORE = 1e4
IN_SPLITS = (256, 256, 768, 768, 1024, 512, 36)
IN_OFFS = tuple(int(v) for v in np.cumsum((0,) + IN_SPLITS))
SCALE = HEAD_DIM ** -0.5

LANES = 128
VMEM_LIMIT_BYTES = 56 * 1024 * 1024

NEG_INF = float("-inf")
N_SLABS = 2 * KV_WIDTH // LANES


def _cparams(semantics):
    return pltpu.CompilerParams(dimension_semantics=semantics, vmem_limit_bytes=VMEM_LIMIT_BYTES)


def _dot(a, b):
    return jnp.dot(a, b, preferred_element_type=F32)


def _dot_nt(a, b):
    return lax.dot_general(a, b, (((1,), (1,)), ((), ())), preferred_element_type=F32)


def _split_dot(x, w_bf16):
    hi = x.astype(BF16)
    lo = (x - hi.astype(F32)).astype(BF16)
    return _dot(hi, w_bf16) + _dot(lo, w_bf16)


def _silu(x):
    return x * jax.nn.sigmoid(x)


def _safe_softmax(s):
    m = jnp.max(s, axis=-1, keepdims=True)
    m = jnp.where(m == NEG_INF, 0.0, m)
    e = jnp.exp(s - m)
    return e / jnp.maximum(jnp.sum(e, axis=-1, keepdims=True), 1e-30)


def _head_norm_rope(v, gain, bd, cos, sin_lo, sin_hi):
    outs = []
    for c in range(v.shape[1] // LANES):
        vc = v[:, c * LANES:(c + 1) * LANES]
        ms = _split_dot(vc * vc, bd)
        n = vc * lax.rsqrt(ms + EPS) * gain[:, c * LANES:(c + 1) * LANES]
        outs.append(n * cos + pltpu.roll(n, LANES - 8, 1) * sin_lo + pltpu.roll(n, 8, 1) * sin_hi)
    return jnp.concatenate(outs, axis=1)


def _head_norm_rope_t(vt, gain_t, cos_t, sin_t):
    outs = []
    for c in range(vt.shape[1] // LANES):
        sl = slice(c * LANES, (c + 1) * LANES)
        v = vt[:, sl].reshape(N_KV_HEADS, HEAD_DIM, LANES)
        ms = jnp.mean(v * v, axis=1, keepdims=True)
        n = v * lax.rsqrt(ms + EPS) * gain_t.reshape(N_KV_HEADS, HEAD_DIM, LANES)
        half = ROPE_DIM // 2
        x1, x2 = n[:, 0:half, :], n[:, half:ROPE_DIM, :]
        cos, sin = cos_t[:, sl][None], sin_t[:, sl][None]
        r = jnp.concatenate([x1 * cos - x2 * sin, x2 * cos + x1 * sin, n[:, ROPE_DIM:, :]], axis=1)
        outs.append(r.reshape(N_KV_HEADS * HEAD_DIM, LANES))
    return jnp.concatenate(outs, axis=1)


def _project_kernel(x_ref, gain_ref, w_ref, wt_ref, qg_ref, kgt_ref, bd_ref, cos_ref, slo_ref, shi_ref, cost_ref, sint_ref,
                    z_ref, sgp_ref, q_ref, sgn_ref, gate_ref, kvt_ref, kvwt_ref, *row_refs):
    x = x_ref[0]
    ms = jnp.mean(x * x, axis=-1, keepdims=True)
    hb = (x * lax.rsqrt(ms + EPS) * gain_ref[...]).astype(BF16)

    def proj(lo, hi):
        return _dot(hb, w_ref[:, lo:hi])

    o = IN_OFFS
    z_ref[0] = proj(o[0], o[1])
    sgp_ref[0] = _silu(proj(o[1], o[2]))
    q_ref[0] = _head_norm_rope(proj(o[2], o[3]), qg_ref[...], bd_ref[...], cos_ref[...], slo_ref[...], shi_ref[...])
    sgn_ref[0] = _silu(proj(o[3], o[4]))
    gate_ref[0] = jax.nn.sigmoid(proj(o[6], o[7]))
    key_gain = {0: 0, 2: 1, 4: 2}
    cos_t, sin_t = cost_ref[...], sint_ref[...]
    for grp in range(6):
        val_t = _dot_nt(wt_ref[grp * KV_WIDTH:(grp + 1) * KV_WIDTH, :], hb)
        if grp in key_gain:
            val_t = _head_norm_rope_t(val_t, kgt_ref[key_gain[grp]], cos_t, sin_t)
        dst, r0 = (kvt_ref, grp * KV_WIDTH) if grp < 4 else (kvwt_ref, (grp - 4) * KV_WIDTH)
        dst[0, r0:r0 + KV_WIDTH, :] = val_t
        if row_refs:
            row_refs[0 if grp < 4 else 1][0, :, r0:r0 + KV_WIDTH] = val_t.T


def _rope_tables(pos):
    half = ROPE_DIM // 2
    freqs = ROPE_THETA ** (-jnp.arange(half, dtype=F32) / half)
    ang = pos.astype(F32)[:, None] * freqs[None, :]
    cos8, sin8 = jnp.cos(ang), jnp.sin(ang)
    p = pos.shape[0]
    rest = HEAD_DIM - ROPE_DIM
    cos = jnp.concatenate([cos8, cos8, jnp.ones((p, rest), F32)], axis=1)
    slo = jnp.concatenate([-sin8, jnp.zeros((p, HEAD_DIM - half), F32)], axis=1)
    shi = jnp.concatenate([jnp.zeros((p, half), F32), sin8, jnp.zeros((p, rest), F32)], axis=1)
    return tuple(jnp.tile(t, (1, LANES // HEAD_DIM)) for t in (cos, slo, shi)) + (cos8.T, sin8.T)


def _project(x, pos, tm, ln_gain, w_bf16, q_norm, k_norm, emit_rows):
    b, s, _ = x.shape
    assert pos.shape[0] == s and tm % LANES == 0
    cos, slo, shi, cos_t, sin_t = _rope_tables(pos)
    tab_spec = pl.BlockSpec((tm, LANES), lambda bb, i: (i, 0))
    tab_t_spec = pl.BlockSpec((ROPE_DIM // 2, tm), lambda bb, i: (0, i))
    bd = jnp.asarray(np.kron(np.eye(LANES // HEAD_DIM), np.full((HEAD_DIM, HEAD_DIM), 1.0 / HEAD_DIM)), BF16)
    qg = jnp.tile(q_norm, N_HEADS)[None, :]
    kgt = jnp.broadcast_to(jnp.tile(k_norm, (1, N_KV_HEADS))[:, :, None], (k_norm.shape[0], KV_WIDTH, LANES))
    gain = ln_gain[None, :]
    wt = w_bf16[:, IN_OFFS[4]:IN_OFFS[6]].T

    def full(a):
        return pl.BlockSpec(a.shape, lambda bb, i: (0,) * a.ndim)

    def rows(w):
        return pl.BlockSpec((1, tm, w), lambda bb, i: (bb, i, 0))

    def cols(f):
        return pl.BlockSpec((1, f, tm), lambda bb, i: (bb, 0, i))

    row_w = (256, 256, 768, 768, 36)
    out_specs = [rows(w) for w in row_w] + [cols(4 * KV_WIDTH), cols(2 * KV_WIDTH)]
    out_shape = ([jax.ShapeDtypeStruct((b, s, w), F32) for w in row_w]
                 + [jax.ShapeDtypeStruct((b, 4 * KV_WIDTH, s), F32), jax.ShapeDtypeStruct((b, 2 * KV_WIDTH, s), F32)])
    if emit_rows:
        out_specs += [rows(4 * KV_WIDTH), rows(2 * KV_WIDTH)]
        out_shape += [jax.ShapeDtypeStruct((b, s, 4 * KV_WIDTH), F32), jax.ShapeDtypeStruct((b, s, 2 * KV_WIDTH), F32)]
    return pl.pallas_call(
        _project_kernel,
        grid=(b, s // tm),
        in_specs=[rows(D_MODEL), full(gain), full(w_bf16), full(wt), full(qg), full(kgt), full(bd),
                  tab_spec, tab_spec, tab_spec, tab_t_spec, tab_t_spec],
        out_specs=out_specs,
        out_shape=out_shape,
        compiler_params=_cparams(("parallel", "parallel")),
        name="project",
    )(x, gain, w_bf16, wt, qg, kgt, bd, cos, slo, shi, cos_t, sin_t)


def _pool_prompt_kernel(z_ref, sgp_ref, wp_ref, scale_ref, out_ref, pad_ref):
    length = z_ref.shape[1]
    pad_ref[0:16, :] = jnp.zeros((16, POOL_WIDTH), F32)
    pad_ref[16:16 + length, :] = z_ref[0]
    z = pad_ref[16:16 + length, :]
    acc = z
    sums = {}
    for k in range(1, POOL_HIST + 1):
        acc = acc + pad_ref[16 - k:16 - k + length, :]
        if k + 1 in POOL_WINDOWS:
            sums[k + 1] = acc
    pos = lax.broadcasted_iota(jnp.int32, (length, 1), 0)
    lane = lax.broadcasted_iota(jnp.int32, (1, POOL_WIDTH), 1)
    means = None
    for gi, w in enumerate(POOL_WINDOWS):
        m = sums[w] / jnp.minimum(w, pos + 1).astype(F32)
        means = m if means is None else jnp.where(lane >= gi * POOL_GROUP_DIM, m, means)
    mixed = _dot((means - z).astype(BF16), wp_ref[...])
    out_ref[0] = mixed * scale_ref[...] * sgp_ref[0]


def _pool_prompt(z, sgp, wp_bd, scale):
    b, length, _ = z.shape
    blk = pl.BlockSpec((1, length, POOL_WIDTH), lambda i: (i, 0, 0))
    return pl.pallas_call(
        _pool_prompt_kernel,
        grid=(b,),
        in_specs=[blk, blk, pl.BlockSpec(wp_bd.shape, lambda i: (0, 0)), pl.BlockSpec(scale.shape, lambda i: (0, 0))],
        out_specs=blk,
        out_shape=jax.ShapeDtypeStruct(z.shape, F32),
        scratch_shapes=[pltpu.VMEM((16 + length, POOL_WIDTH), F32)],
        compiler_params=_cparams(("parallel",)),
        name="pool_prompt",
    )(z, sgp, wp_bd, scale)


def _compress_accumulate(load_rows, w1_ref, slot, nj):
    acc = jnp.zeros((nj, 2 * LANES), F32)
    for s2 in range(CMP_STRIDE // 2):
        lhs = jnp.concatenate([load_rows(2 * s2), load_rows(2 * s2 + 1)], axis=1).astype(BF16)
        acc = acc + _dot(lhs, w1_ref[slot, s2])
    return acc


def _compress_finish(acc, b1, w2):
    nj = acc.shape[0]
    second = pltpu.roll(acc[:, LANES:], nj - 1, 0)
    hid = _silu(acc[:, :LANES] + second + b1)
    return _dot(hid.astype(BF16), w2)


def _compress_prompt_kernel(kvt_ref, w1_ref, b1_ref, w2_ref, kc_ref, vc_ref, xbuf):
    length = kvt_ref.shape[2]
    nj = length // CMP_STRIDE
    for sl in range(N_SLABS):
        for tt in range(length // LANES):
            xbuf[sl, tt * LANES:(tt + 1) * LANES, :] = kvt_ref[0, sl * LANES:(sl + 1) * LANES, tt * LANES:(tt + 1) * LANES].T
    for sl in range(N_SLABS):
        slot, out_ref = sl // 2, (kc_ref, vc_ref)[sl // 2]
        acc = _compress_accumulate(lambda s, sl=sl: xbuf[sl, pl.ds(s, nj, stride=CMP_STRIDE), :], w1_ref, slot, nj)
        out_ref[0, :, (sl % 2) * LANES:(sl % 2 + 1) * LANES] = _compress_finish(acc, b1_ref[slot], w2_ref[slot])


def _compress_prompt(kvt, w1cat, b1t, w2bd):
    b, _, length = kvt.shape
    nj = length // CMP_STRIDE

    def full(a):
        return pl.BlockSpec(a.shape, lambda bb: (0,) * a.ndim)

    out_blk = pl.BlockSpec((1, nj, KV_WIDTH), lambda bb: (bb, 0, 0))
    return pl.pallas_call(
        _compress_prompt_kernel,
        grid=(b,),
        in_specs=[pl.BlockSpec((1, 2 * KV_WIDTH, length), lambda bb: (bb, 0, 0)), full(w1cat), full(b1t), full(w2bd)],
        out_specs=[out_blk, out_blk],
        out_shape=[jax.ShapeDtypeStruct((b, nj, KV_WIDTH), F32)] * 2,
        scratch_shapes=[pltpu.VMEM((N_SLABS, length, LANES), F32)],
        compiler_params=_cparams(("parallel",)),
        name="compress_prompt",
    )(kvt, w1cat, b1t, w2bd)


def _compress_weights(w_cmp1, b_cmp1, w_cmp2):
    eye2 = jnp.eye(LANES // HEAD_DIM, dtype=F32)

    def bd2(m):
        return jnp.kron(eye2, m)

    w1cat = []
    for slot in range(2):
        per = []
        for s2 in range(CMP_STRIDE // 2):
            blocks = []
            for s in (2 * s2, 2 * s2 + 1):
                blocks.append(jnp.concatenate([bd2(w_cmp1[slot, s]), bd2(w_cmp1[slot, CMP_STRIDE + s])], axis=1))
            per.append(jnp.concatenate(blocks, axis=0))
        w1cat.append(jnp.stack(per))
    w1cat = jnp.stack(w1cat).astype(BF16)
    b1t = jnp.tile(b_cmp1, (1, LANES // HEAD_DIM))[:, None, :]
    w2bd = jnp.stack([bd2(w_cmp2[0]), bd2(w_cmp2[1])]).astype(BF16)
    return w1cat, b1t, w2bd


def _overlap_matrix(n_cmp_pad, n_cmp, n_sel, n_sel_pad):
    cstart = np.arange(n_cmp_pad)[:, None] * CMP_STRIDE
    bstart = np.arange(n_sel_pad)[None, :] * SEL_BLOCK
    ov = np.clip(np.minimum(cstart + CMP_BLOCK, bstart + SEL_BLOCK) - np.maximum(cstart, bstart), 0, None) / CMP_BLOCK
    ov = ov * (np.arange(n_cmp_pad)[:, None] < n_cmp) * (np.arange(n_sel_pad)[None, :] < n_sel)
    return jnp.asarray(ov, BF16)


def _topk_mask(score, n_iter):
    lane = lax.broadcasted_iota(jnp.int32, score.shape, 1)
    rank = jnp.zeros(score.shape, F32)
    for j in range(n_iter):
        col = score[:, j:j + 1]
        ahead = (col > score) | ((col == score) & (lane > j))
        rank = rank + jnp.where(ahead, 1.0, 0.0)
    return rank, (rank < SEL_TOP_N) & (score > NEG_INF)


TQ = 256
WIN_TILES = WINDOW // TQ + 1
Q_GROUPS = 4


def _topk_mask_t(score_t):
    row = lax.broadcasted_iota(jnp.int32, score_t.shape, 0)
    rank = jnp.zeros(score_t.shape, F32)
    for j in range(score_t.shape[0]):
        cur = score_t[j:j + 1, :]
        ahead = (cur > score_t) | ((cur == score_t) & (row > j))
        rank = rank + jnp.where(ahead, 1.0, 0.0)
    return (rank < SEL_TOP_N) & (score_t > NEG_INF)


def _attend_post(s, v_t, half):
    m = jnp.max(s, axis=-1, keepdims=True)
    m = jnp.where(m == NEG_INF, 0.0, m)
    e = jnp.exp(s - m)
    row_half = lax.broadcasted_iota(jnp.int32, (v_t.shape[0], 1), 0) // HEAD_DIM
    out = _dot_nt(e.astype(BF16), jnp.where(row_half == half, v_t, jnp.ones_like(v_t)))
    return out / jnp.maximum(pltpu.roll(out, HEAD_DIM, 1), 1e-30)


def _nsa_prompt_kernel(*refs, tile0):
    q_ref, gate_ref, kc_ref, vc_ref, kst_ref, vst_ref = refs[:6]
    win_refs, (ovt_ref, ex_ref, out_ref) = refs[6:6 + WIN_TILES], refs[6 + WIN_TILES:]
    i = tile0 + pl.program_id(1)
    nk = kst_ref.shape[2]
    n_cmp = kc_ref.shape[1]
    n_sel = ex_ref.shape[0]
    qpos = i * TQ + lax.broadcasted_iota(jnp.int32, (TQ, 1), 0)
    qpos_l = i * TQ + lax.broadcasted_iota(jnp.int32, (1, TQ), 1)
    lane = lax.broadcasted_iota(jnp.int32, (1, LANES), 1)
    lane_half = lane // HEAD_DIM

    ncol = lax.broadcasted_iota(jnp.int32, (1, n_cmp), 1)
    bias_c = jnp.where(ncol * CMP_STRIDE + CMP_BLOCK - 1 <= qpos, 0.0, NEG_INF)
    bias_w = []
    for u in range(WIN_TILES):
        kb = i - (WIN_TILES - 1) + u
        rel = qpos - (kb * TQ + lax.broadcasted_iota(jnp.int32, (1, TQ), 1))
        bias_w.append(jnp.where((kb >= 0) & (rel >= 0) & (rel < WINDOW), 0.0, NEG_INF))
    bias_w = jnp.concatenate(bias_w, axis=1)
    jrow = lax.broadcasted_iota(jnp.int32, (n_sel, 1), 0)
    qblk_l = qpos_l // SEL_BLOCK
    valid_t = jrow * SEL_BLOCK <= qpos_l
    forced_t = (jrow == 0) | (jrow == qblk_l) | (jrow == qblk_l - 1)
    causal = lax.broadcasted_iota(jnp.int32, (1, nk), 1) <= qpos
    gates = gate_ref[0]

    def add_bias(s, bias):
        return (s.reshape(GQA, TQ, s.shape[-1]) + bias[None]).reshape(GQA * TQ, s.shape[-1])

    heads = range(N_KV_HEADS)
    span = [((h // 2) * LANES, (h // 2 + 1) * LANES) for h in heads]

    qzs = []
    for h in heads:
        rows = []
        for g in range(GQA):
            qh = h * GQA + g
            x = q_ref[0, :, (qh // 2) * LANES:(qh // 2 + 1) * LANES] * SCALE
            if qh % 2 != h % 2:
                x = pltpu.roll(x, HEAD_DIM, 1)
            rows.append(jnp.where(lane_half == h % 2, x, 0.0))
        qzs.append(jnp.concatenate(rows, axis=0).astype(BF16))

    o_cmps, score_ts = [], []
    for h in heads:
        lo, hi = span[h]
        p_c = _safe_softmax(add_bias(_dot_nt(qzs[h], kc_ref[0, :, lo:hi].astype(BF16)), bias_c))
        o_cmps.append(_dot(p_c.astype(BF16), vc_ref[0, :, lo:hi].astype(BF16)))
        p_sum = p_c[0:TQ] + p_c[TQ:2 * TQ] + p_c[2 * TQ:3 * TQ]
        p_hi = p_sum.astype(BF16)
        p_lo = (p_sum - p_hi.astype(F32)).astype(BF16)
        imp_t = _dot_nt(ovt_ref[...], p_hi) + _dot_nt(ovt_ref[...], p_lo)
        score_ts.append(jnp.where(valid_t, jnp.where(forced_t, FORCE_SCORE, imp_t), NEG_INF))

    sels, o_wins = [], []
    for h in heads:
        lo, hi = span[h]
        sels.append(jnp.where(_topk_mask_t(score_ts[h]), 1.0, 0.0).T.astype(BF16))
        kwt = jnp.concatenate([w[0, lo:hi, :] for w in win_refs], axis=1).astype(BF16)
        vwt = jnp.concatenate([w[0, KV_WIDTH + lo:KV_WIDTH + hi, :] for w in win_refs], axis=1).astype(BF16)
        o_wins.append(_attend_post(add_bias(_dot(qzs[h], kwt), bias_w), vwt, h % 2))

    o_slcs = []
    for h in heads:
        lo, hi = span[h]
        selw = _dot(sels[h], ex_ref[...])
        bias_s = jnp.where((selw > 0.5) & causal, 0.0, NEG_INF)
        o_slcs.append(_attend_post(add_bias(_dot(qzs[h], kst_ref[0, lo:hi, :].astype(BF16)), bias_s),
                                   vst_ref[0, lo:hi, :].astype(BF16), h % 2))

    out_chunks = [jnp.zeros((TQ, LANES), F32) for _ in range(NSA_WIDTH // LANES)]
    for h in heads:
        half, o_cmp, o_slc, o_win = h % 2, o_cmps[h], o_slcs[h], o_wins[h]
        for g in range(GQA):
            qh = h * GQA + g
            gi = qh * 3
            sl = slice(g * TQ, (g + 1) * TQ)
            o = (gates[:, gi:gi + 1] * o_cmp[sl] + gates[:, gi + 1:gi + 2] * o_slc[sl] + gates[:, gi + 2:gi + 3] * o_win[sl])
            if qh % 2 != half:
                o = pltpu.roll(o, HEAD_DIM, 1)
            out_chunks[qh // 2] = out_chunks[qh // 2] + jnp.where(lane_half == qh % 2, o, 0.0)
    for c, val in enumerate(out_chunks):
        out_ref[0, :, c * LANES:(c + 1) * LANES] = val


def _nsa_prompt(q, gates, kc, vc, kvt, kvwt):
    b, seq, _ = q.shape
    n_cmp = kc.shape[1]
    n_sel = seq // SEL_BLOCK
    n_tiles = seq // TQ
    assert n_sel % 8 == 0 and n_tiles % Q_GROUPS == 0 and (seq // Q_GROUPS) % LANES == 0
    ovt = _overlap_matrix(n_cmp, (seq - CMP_BLOCK) // CMP_STRIDE + 1, n_sel, n_sel).T
    ex = np.arange(n_sel)[:, None] == (np.arange(seq)[None, :] // SEL_BLOCK)
    tiles_per_group = n_tiles // Q_GROUPS

    outs = []
    for grp in range(Q_GROUPS):
        tile0 = grp * tiles_per_group
        nk = (grp + 1) * (seq // Q_GROUPS)
        ex_g = jnp.asarray(ex[:, :nk], BF16)

        def tile(w, tile0=tile0):
            return pl.BlockSpec((1, TQ, w), lambda bb, i: (bb, tile0 + i, 0))

        def whole(rows):
            return pl.BlockSpec((1, rows, KV_WIDTH), lambda bb, i: (bb, 0, 0))

        def feat_rows(row_blk, nk=nk):
            return pl.BlockSpec((1, KV_WIDTH, nk), lambda bb, i, rb=row_blk: (bb, rb, 0))

        def win_tile(u, tile0=tile0):
            return pl.BlockSpec((1, 2 * KV_WIDTH, TQ),
                                lambda bb, i, u=u: (bb, 0, jnp.maximum(tile0 + i - (WIN_TILES - 1) + u, 0)))

        def full(a):
            return pl.BlockSpec(a.shape, lambda bb, i: (0,) * a.ndim)

        in_specs = ([tile(NSA_WIDTH), tile(gates.shape[-1]), whole(n_cmp), whole(n_cmp), feat_rows(2), feat_rows(3)]
                    + [win_tile(u) for u in range(WIN_TILES)] + [full(ovt), full(ex_g)])
        args = [q, gates, kc, vc, kvt, kvt] + [kvwt] * WIN_TILES + [ovt, ex_g]
        outs.append(pl.pallas_call(
            functools.partial(_nsa_prompt_kernel, tile0=tile0),
            grid=(b, tiles_per_group),
            in_specs=in_specs,
            out_specs=pl.BlockSpec((1, TQ, NSA_WIDTH), lambda bb, i: (bb, i, 0)),
            out_shape=jax.ShapeDtypeStruct((b, seq // Q_GROUPS, NSA_WIDTH), F32),
            compiler_params=_cparams(("parallel", "arbitrary")),
            name=f"nsa_prompt_g{grp}",
        )(*args))
    return outs


def _out_proj_kernel(x_ref, gp_ref, o_ref, sgn_ref, w_ref, y_ref):
    gn = (o_ref[...] * sgn_ref[...]).astype(BF16)
    y_ref[...] = (x_ref[...] + _dot(gp_ref[...].astype(BF16), w_ref[0:POOL_WIDTH, :])
                  + _dot(gn, w_ref[POOL_WIDTH:, :]))


def _out_proj_groups_kernel(x_ref, gp_ref, sgn_ref, w_ref, *rest):
    o_refs, y_ref = rest[:-1], rest[-1]
    g = pl.program_id(1)
    o = o_refs[0][0]
    for k in range(1, len(o_refs)):
        o = jnp.where(g == k, o_refs[k][0], o)
    gn = (o * sgn_ref[0]).astype(BF16)
    y_ref[0] = (x_ref[0] + _dot(gp_ref[0].astype(BF16), w_ref[0:POOL_WIDTH, :]) + _dot(gn, w_ref[POOL_WIDTH:, :]))


def _out_proj_groups(x, gp, o_groups, sgn, w_bf16):
    b, seq, _ = x.shape
    n_groups = len(o_groups)
    tm = seq // n_groups

    def rows(w):
        return pl.BlockSpec((1, tm, w), lambda bb, g: (bb, g, 0))

    grp = pl.BlockSpec((1, tm, NSA_WIDTH), lambda bb, g: (bb, 0, 0))
    return pl.pallas_call(
        _out_proj_groups_kernel,
        grid=(b, n_groups),
        in_specs=[rows(D_MODEL), rows(POOL_WIDTH), rows(NSA_WIDTH), pl.BlockSpec(w_bf16.shape, lambda bb, g: (0, 0))]
        + [grp] * n_groups,
        out_specs=rows(D_MODEL),
        out_shape=jax.ShapeDtypeStruct((b, seq, D_MODEL), F32),
        compiler_params=_cparams(("parallel", "arbitrary")),
        name="out_proj_prompt",
    )(x, gp, sgn, w_bf16, *o_groups)


def _out_proj(x, gp, o_nsa, sgn, w_bf16, tm):
    t = x.shape[0]

    def rows(w):
        return pl.BlockSpec((tm, w), lambda i: (i, 0))

    return pl.pallas_call(
        _out_proj_kernel,
        grid=(t // tm,),
        in_specs=[rows(D_MODEL), rows(POOL_WIDTH), rows(NSA_WIDTH), rows(NSA_WIDTH),
                  pl.BlockSpec(w_bf16.shape, lambda i: (0, 0))],
        out_specs=rows(D_MODEL),
        out_shape=jax.ShapeDtypeStruct((t, D_MODEL), F32),
        compiler_params=_cparams(("parallel",)),
        name="out_proj",
    )(x, gp, o_nsa, sgn, w_bf16)


ROWS = GQA * 8
CH_PAGES = 32


def _page_copy(cache_ref, xt, sem, page, p, slot):
    return pltpu.make_async_copy(cache_ref.at[page, pl.ds(0, 2 * KV_WIDTH), :], xt.at[slot, p], sem.at[slot])


def _compress_decode_kernel(pt_ref, cache_ref, qbd_ref, perm_ref, w1_ref, b1_ref, w2_ref, ov_ref, ocmp_ref, idx_ref,
                            xt, xs, fg, sem, *, n_pages, page_size, n_chunks, nch):
    t = pl.program_id(0)
    bpp = page_size // CMP_STRIDE
    jch = CH_PAGES * bpp
    nj = nch * jch
    past = n_pages * page_size

    def chunk_dma(step, slot, wait):
        base = (step // nch) * n_pages + (step % nch) * CH_PAGES

        def body(p, carry):
            cp = _page_copy(cache_ref, xt, sem, pt_ref[base + p], p, slot)
            cp.wait() if wait else cp.start()
            return carry
        lax.fori_loop(0, CH_PAGES, body, 0, unroll=8)

    @pl.when(t == 0)
    def _():
        chunk_dma(0, 0, wait=False)
        xs[...] = jnp.zeros(xs.shape, BF16)

    @pl.when(t + 1 < n_chunks)
    def _():
        chunk_dma(t + 1, (t + 1) % 2, wait=False)

    slot = t % 2

    @pl.when(t < n_chunks)
    def _():
        chunk_dma(t, slot, wait=True)

    for pp in range(CH_PAGES // 2):
        ys = [_dot_nt(perm_ref[...], xt[slot, 2 * pp + k].astype(BF16)) for k in range(2)]
        for s in range(CMP_STRIDE):
            rows = jnp.concatenate([y[s * bpp:(s + 1) * bpp, :] for y in ys], axis=0)
            xs[slot, s, pp * 2 * bpp:(pp + 1) * 2 * bpp, :] = rows.astype(BF16)

    prev = 1 - slot
    row0 = pl.multiple_of(((t + nch - 1) % nch) * jch, jch)
    for sl in range(N_SLABS):
        acc = jnp.zeros((jch, 2 * LANES), F32)
        for s2 in range(CMP_STRIDE // 2):
            lhs = jnp.concatenate([xs[prev, 2 * s2, :, sl * LANES:(sl + 1) * LANES],
                                   xs[prev, 2 * s2 + 1, :, sl * LANES:(sl + 1) * LANES]], axis=1)
            acc = acc + _dot(lhs, w1_ref[sl // 2, s2])
        fg[sl, pl.ds(row0, jch), :] = acc

    @pl.when((t > 0) & (t % nch == 0))
    def _():
        kcv = [_compress_finish(fg[sl], b1_ref[sl // 2], w2_ref[sl // 2]) for sl in range(N_SLABS)]
        kc = jnp.concatenate(kcv[0:2], axis=1).astype(BF16)
        vc = jnp.concatenate(kcv[2:4], axis=1).astype(BF16)
        s = _dot_nt(qbd_ref[0].astype(BF16), kc)
        ncol = lax.broadcasted_iota(jnp.int32, (1, nj), 1)
        s = s + jnp.where(ncol * CMP_STRIDE + CMP_BLOCK - 1 <= past, 0.0, NEG_INF)
        p = _safe_softmax(s)
        ocmp_ref[0] = _dot(p.astype(BF16), vc)
        p_sum = p[0:8] + p[8:16] + p[16:24]
        imp = _split_dot(p_sum, ov_ref[...])
        lane = lax.broadcasted_iota(jnp.int32, (1, imp.shape[1]), 1)
        qblk = past // SEL_BLOCK
        forced = (lane == 0) | (lane == qblk) | (lane == qblk - 1)
        score = jnp.where(lane * SEL_BLOCK <= past, jnp.where(forced, FORCE_SCORE, imp), NEG_INF)
        rank, sel = _topk_mask(score, qblk + 1)
        lanef = lane.astype(F32)
        lane16 = lax.broadcasted_iota(jnp.int32, (1, SEL_TOP_N), 1)
        idx = jnp.zeros((8, SEL_TOP_N), F32)
        for r in range(SEL_TOP_N):
            v = jnp.sum(jnp.where((rank == r) & sel, lanef, 0.0), axis=-1, keepdims=True)
            idx = jnp.where(lane16 == r, v, idx)
        idx_ref[0] = idx.astype(jnp.int32)


def _compress_decode(page_table, cache_t, qbd, w1cat, b1t, w2bd):
    nb, n_pages = page_table.shape
    page_size = cache_t.shape[2]
    assert n_pages % CH_PAGES == 0 and page_size % LANES == 0
    nch = n_pages // CH_PAGES
    past = n_pages * page_size
    nj = past // CMP_STRIDE
    n_sel = past // SEL_BLOCK + 1
    assert n_sel > SEL_TOP_N
    n_sel_pad = -(-n_sel // LANES) * LANES
    ov = _overlap_matrix(nj, (past + 1 - CMP_BLOCK) // CMP_STRIDE + 1, n_sel, n_sel_pad)
    bpp = page_size // CMP_STRIDE
    tok = np.arange(page_size)
    perm = jnp.asarray(((tok % CMP_STRIDE) * bpp + tok // CMP_STRIDE)[None, :] == np.arange(page_size)[:, None], BF16)

    def full(a):
        return pl.BlockSpec(a.shape, lambda t, pt: (0,) * a.ndim)

    def per_b(shape):
        return pl.BlockSpec((1,) + shape, lambda t, pt: (jnp.maximum(t - 1, 0) // nch, 0, 0))

    n_chunks = nb * nch
    grid_spec = pltpu.PrefetchScalarGridSpec(
        num_scalar_prefetch=1,
        grid=(n_chunks + 1,),
        in_specs=[pl.BlockSpec(memory_space=pl.ANY), per_b((ROWS, KV_WIDTH)), full(perm), full(w1cat), full(b1t),
                  full(w2bd), full(ov)],
        out_specs=[per_b((ROWS, KV_WIDTH)), per_b((8, SEL_TOP_N))],
        scratch_shapes=[pltpu.VMEM((2, CH_PAGES, 2 * KV_WIDTH, page_size), F32),
                        pltpu.VMEM((2, CMP_STRIDE, CH_PAGES * bpp, 2 * KV_WIDTH), BF16),
                        pltpu.VMEM((N_SLABS, nj, 2 * LANES), F32),
                        pltpu.SemaphoreType.DMA((2,))],
    )
    return pl.pallas_call(
        functools.partial(_compress_decode_kernel, n_pages=n_pages, page_size=page_size, n_chunks=n_chunks, nch=nch),
        grid_spec=grid_spec,
        out_shape=[jax.ShapeDtypeStruct((nb, ROWS, KV_WIDTH), F32), jax.ShapeDtypeStruct((nb, 8, SEL_TOP_N), jnp.int32)],
        compiler_params=_cparams(("arbitrary",)),
        name="compress_decode",
    )(page_table.reshape(-1), cache_t, qbd, perm, w1cat, b1t, w2bd, ov)


def _block_copy(cache_ref, kvbuf, sem, pt_ref, idx_ref, bb, h, r, slot, n_pages, page_size):
    bpp = page_size // SEL_BLOCK
    j = idx_ref[(bb * N_KV_HEADS + h) * SEL_TOP_N + r]
    jj = jnp.minimum(j, n_pages * bpp - 1)
    page = pt_ref[bb * n_pages + jj // bpp]
    dst0 = pl.multiple_of(r * page_size, page_size)
    return pltpu.make_async_copy(cache_ref.at[page, pl.ds(2, 2), h],
                                 kvbuf.at[slot, h, :, :, pl.ds(dst0, page_size)], sem.at[slot])


def _heads_to_lanes64(x, rowh):
    out = jnp.zeros((x.shape[0], HEAD_DIM), F32)
    for h in range(N_KV_HEADS):
        out = jnp.where(rowh == h, x[:, h * HEAD_DIM:(h + 1) * HEAD_DIM], out)
    return out


def _nsa_decode_kernel(pt_ref, idx_ref, cache_ref, qbd_ref, q4_ref, kvn_ref, kvwn_ref, kvwnt_ref, wint_ref, gate_ref,
                       ocmp_ref, idxv_ref, ex_ref, out_ref, winout_ref, kvbuf, sem, *, n_pages, page_size):
    b = pl.program_id(0)
    nb = pl.num_programs(0)
    bpp = page_size // SEL_BLOCK
    n_past_blocks = n_pages * bpp
    nk = SEL_TOP_N * page_size
    win_buf = wint_ref.shape[2]

    def gather(bb, slot, wait):
        for h in range(N_KV_HEADS):
            def body(r, carry, h=h):
                cp = _block_copy(cache_ref, kvbuf, sem, pt_ref, idx_ref, bb, h, r, slot, n_pages, page_size)
                cp.wait() if wait else cp.start()
                return carry
            lax.fori_loop(0, SEL_TOP_N, body, 0)

    @pl.when(b == 0)
    def _():
        gather(0, 0, wait=False)

    nxt, nslot = (b + 1) % nb, (b + 1) % 2
    slot = b % 2
    gather(b, slot, wait=True)

    qbd = qbd_ref[0]
    rowh = lax.broadcasted_iota(jnp.int32, (ROWS, 1), 0) % 8
    kvn = kvn_ref[0]
    k_new, v_new = kvn[:, 2 * KV_WIDTH:3 * KV_WIDTH], kvn[:, 3 * KV_WIDTH:]
    jcol = _dot(idxv_ref[0].astype(F32).astype(BF16), ex_ref[...]).astype(jnp.int32)
    col_blk = (lax.broadcasted_iota(jnp.int32, (1, nk), 1) % page_size) // SEL_BLOCK
    in_cache = jcol < n_past_blocks
    bias_all = jnp.where(in_cache & (col_blk == jcol % bpp), 0.0, NEG_INF)
    new_bias = jnp.where(jnp.max(jcol, axis=-1, keepdims=True) >= n_past_blocks, 0.0, NEG_INF)
    new_bias = jnp.concatenate([new_bias] * GQA, axis=0)

    s_new = jnp.sum(qbd * k_new, axis=-1, keepdims=True)
    o_slc = jnp.zeros((ROWS, HEAD_DIM), F32)
    for h in range(N_KV_HEADS):
        s = _dot(q4_ref[0, h].astype(BF16), kvbuf[slot, h, 0].astype(BF16)) + bias_all[h:h + 1, :]
        sn = s_new + new_bias
        m = jnp.maximum(jnp.max(s, axis=-1, keepdims=True), sn)
        m = jnp.where(m == NEG_INF, 0.0, m)
        e, en = jnp.exp(s - m), jnp.exp(sn - m)
        den = jnp.maximum(jnp.sum(e, axis=-1, keepdims=True) + en, 1e-30)
        pv = _dot_nt(e.astype(BF16), kvbuf[slot, h, 1].astype(BF16))
        o_h = (pv + en * v_new[:, h * HEAD_DIM:(h + 1) * HEAD_DIM]) / den
        o_slc = jnp.where(rowh == h, o_h, o_slc)
        for r in range(SEL_TOP_N):
            _block_copy(cache_ref, kvbuf, sem, pt_ref, idx_ref, nxt, h, r, nslot, n_pages, page_size).start()

    kwn = kvwn_ref[0]
    s = _dot(qbd.astype(BF16), wint_ref[0, 0:KV_WIDTH, :].astype(BF16))
    rel = win_buf - lax.broadcasted_iota(jnp.int32, (1, win_buf), 1)
    s = s + jnp.where((rel >= 0) & (rel < WINDOW), 0.0, NEG_INF)
    sn = jnp.sum(qbd * kwn[:, :KV_WIDTH], axis=-1, keepdims=True)
    m = jnp.maximum(jnp.max(s, axis=-1, keepdims=True), sn)
    e, en = jnp.exp(s - m), jnp.exp(sn - m)
    den = jnp.sum(e, axis=-1, keepdims=True) + en
    o_w = (_dot_nt(e.astype(BF16), wint_ref[0, KV_WIDTH:, :].astype(BF16)) + en * kwn[:, KV_WIDTH:]) / den

    gates = gate_ref[0]
    out_ref[0] = (gates[:, 0:1] * _heads_to_lanes64(ocmp_ref[0], rowh) + gates[:, 1:2] * o_slc
                  + gates[:, 2:3] * _heads_to_lanes64(o_w, rowh))

    seq_lane = lax.broadcasted_iota(jnp.int32, (1, kvwnt_ref.shape[1]), 1)
    new_col = jnp.sum(jnp.where(seq_lane == b, kvwnt_ref[...], 0.0), axis=-1, keepdims=True)
    row_lane = lax.broadcasted_iota(jnp.int32, (1, win_buf), 1)
    winout_ref[0] = jnp.where(row_lane == win_buf - 1, new_col, pltpu.roll(wint_ref[0], win_buf - 1, 1))

    @pl.when(b == nb - 1)
    def _():
        gather(nxt, nslot, wait=True)


def _nsa_decode(page_table, idx8, cache_t, qbd, q4, kvs, kvws, kvws_t, win_t, gate_rows, ocmp):
    nb, n_pages = page_table.shape
    idx = idx8[:, :N_KV_HEADS, :]
    page_size = cache_t.shape[2]
    assert page_size % SEL_BLOCK == 0 and page_size % LANES == 0
    cache5 = cache_t.reshape(cache_t.shape[0], 4, N_KV_HEADS, HEAD_DIM, page_size)

    def per_b(shape):
        return pl.BlockSpec((1,) + shape, lambda bb, pt, ix: (bb,) + (0,) * len(shape))

    nk = SEL_TOP_N * page_size
    ex = jnp.asarray(np.arange(SEL_TOP_N)[:, None] == (np.arange(nk)[None, :] // page_size), BF16)
    grid_spec = pltpu.PrefetchScalarGridSpec(
        num_scalar_prefetch=2,
        grid=(nb,),
        in_specs=[pl.BlockSpec(memory_space=pl.ANY), per_b((ROWS, KV_WIDTH)), per_b((N_KV_HEADS, ROWS, HEAD_DIM)),
                  per_b((1, 4 * KV_WIDTH)), per_b((1, 2 * KV_WIDTH)),
                  pl.BlockSpec(kvws_t.shape, lambda bb, pt, ix: (0, 0)), per_b(win_t.shape[1:]), per_b((ROWS, 3)),
                  per_b((ROWS, KV_WIDTH)), per_b((8, SEL_TOP_N)), pl.BlockSpec(ex.shape, lambda bb, pt, ix: (0, 0))],
        out_specs=[per_b((ROWS, HEAD_DIM)), per_b(win_t.shape[1:])],
        scratch_shapes=[pltpu.VMEM((2, N_KV_HEADS, 2, HEAD_DIM, nk), F32),
                        pltpu.SemaphoreType.DMA((2,))],
    )
    return pl.pallas_call(
        functools.partial(_nsa_decode_kernel, n_pages=n_pages, page_size=page_size),
        grid_spec=grid_spec,
        out_shape=[jax.ShapeDtypeStruct((nb, ROWS, HEAD_DIM), F32), jax.ShapeDtypeStruct(win_t.shape, F32)],
        compiler_params=_cparams(("arbitrary",)),
        name="nsa_decode",
    )(page_table.reshape(-1), idx.reshape(-1), cache5, qbd, q4, kvs[:, None, :], kvws[:, None, :], kvws_t, win_t,
      gate_rows, ocmp, idx8, ex)


def _pool_sample_kernel(hist_ref, z_ref, sgp_ref, wp_ref, scale_ref, out_ref, *, pos):
    z = z_ref[...]
    acc = z
    sums = {}
    for k in range(1, POOL_HIST + 1):
        acc = acc + hist_ref[POOL_HIST - k]
        if k + 1 in POOL_WINDOWS:
            sums[k + 1] = acc
    lane = lax.broadcasted_iota(jnp.int32, (1, POOL_WIDTH), 1)
    means = None
    for gi, w in enumerate(POOL_WINDOWS):
        m = sums[w] / float(min(w, pos + 1))
        means = m if means is None else jnp.where(lane >= gi * POOL_GROUP_DIM, m, means)
    mixed = _dot((means - z).astype(BF16), wp_ref[...])
    out_ref[...] = mixed * scale_ref[...] * sgp_ref[...]


def _pool_sample(hist_t, z, sgp, wp_bd, scale, pos):
    return pl.pallas_call(
        functools.partial(_pool_sample_kernel, pos=pos),
        out_shape=jax.ShapeDtypeStruct(z.shape, F32),
        name="pool_sample",
    )(hist_t, z, sgp, wp_bd, scale)


def _prep_weights(w_in, w_cmp1, b_cmp1, w_cmp2, w_pool, pool_scale, w_out):
    w1cat, b1t, w2bd = _compress_weights(w_cmp1, b_cmp1, w_cmp2)
    wp_bd = jax.scipy.linalg.block_diag(*[w_pool[g] for g in range(len(POOL_WINDOWS))]).astype(BF16)
    return dict(w_in=w_in.astype(BF16), w_out=w_out.astype(BF16), w1cat=w1cat, b1t=b1t, w2bd=w2bd,
                wp_bd=wp_bd, pool_scale=pool_scale[None, :])


def _rows_layout(x4):
    x = jnp.swapaxes(x4, 1, 2)
    pad = [(0, 0)] * x.ndim
    pad[2] = (0, 8 - N_KV_HEADS)
    x = jnp.pad(x, pad)
    return x.reshape((x.shape[0], ROWS) + x.shape[3:])


def _token_minor_to_logical(xt, lead):
    t = xt.shape[-1]
    x = xt.reshape(lead + (-1, N_KV_HEADS, HEAD_DIM, t))
    n = len(lead)
    return jnp.transpose(x, tuple(range(n)) + (n + 3, n, n + 1, n + 2))


def kernel(x_prompt, x_sample, cache_kv, state_kv_win, state_pool, page_table, ln_gain, w_in, q_norm, k_norm, w_cmp1, b_cmp1, w_cmp2, w_pool, pool_scale, w_out):
    assert cache_kv.shape[0] == 1 and x_sample.shape[1] == 1
    b, seq, _ = x_prompt.shape
    nb = x_sample.shape[0]
    n_pages, page_size = page_table.shape[1], cache_kv.shape[2]
    past = n_pages * page_size
    win_buf = state_kv_win.shape[2]
    w = _prep_weights(w_in[0], w_cmp1[0], b_cmp1[0], w_cmp2[0], w_pool[0], pool_scale[0], w_out[0])

    z, sgp, q, sgn, gates, kvt, kvwt = _project(x_prompt, jnp.arange(seq), 512, ln_gain[0], w["w_in"], q_norm[0],
                                                 k_norm[0], emit_rows=False)
    gp = _pool_prompt(z, sgp, w["wp_bd"], w["pool_scale"])
    kc, vc = _compress_prompt(kvt, w["w1cat"], w["b1t"], w["w2bd"])
    o_groups = _nsa_prompt(q, gates, kc, vc, kvt, kvwt)
    y_p = _out_proj_groups(x_prompt, gp, o_groups, sgn, w["w_out"])

    xs = x_sample.reshape(1, nb, D_MODEL)
    zs, sgps, qs, sgns, gates_s, kvst, kvwst, kvs, kvws = [
        a[0] for a in _project(xs, jnp.full((nb,), past), nb, ln_gain[0], w["w_in"], q_norm[0], k_norm[0], emit_rows=True)]
    cache_t = jnp.transpose(cache_kv[0], (0, 2, 3, 4, 1)).reshape(cache_kv.shape[1], 4 * KV_WIDTH, page_size)
    win_t = jnp.transpose(state_kv_win[0], (0, 2, 3, 4, 1)).reshape(nb, 2 * KV_WIDTH, win_buf)
    q4 = qs.reshape(nb, N_KV_HEADS, GQA, HEAD_DIM) * SCALE
    eye = jnp.eye(N_KV_HEADS, dtype=F32)
    qbd = _rows_layout(jnp.einsum("bhgd,hk->bhgkd", q4, eye).reshape(nb, N_KV_HEADS, GQA, KV_WIDTH))
    q4h = jnp.swapaxes(_rows_layout(jnp.einsum("bhgd,hk->bhgkd", q4, eye)), 1, 2)
    ocmp, idx = _compress_decode(page_table, cache_t, qbd, w["w1cat"], w["b1t"], w["w2bd"])
    gate_rows = _rows_layout(gates_s.reshape(nb, N_KV_HEADS, GQA, 3))
    o_rows, win_new_t = _nsa_decode(page_table, idx, cache_t, qbd, q4h, kvs, kvws, kvwst, win_t, gate_rows, ocmp)
    o_s = jnp.swapaxes(o_rows.reshape(nb, GQA, 8, HEAD_DIM)[:, :, :N_KV_HEADS], 1, 2).reshape(nb, NSA_WIDTH)
    gps = _pool_sample(jnp.swapaxes(state_pool[0], 0, 1), zs, sgps, w["wp_bd"], w["pool_scale"], past)
    y_s = _out_proj(x_sample.reshape(nb, D_MODEL), gps, o_s, sgns, w["w_out"], nb)

    kv_prompt = _token_minor_to_logical(kvt, (b,))[None]
    kv_sample = _token_minor_to_logical(kvst, ())[None, :, None]
    win_keep = min(WINDOW, seq)
    win_prompt = _token_minor_to_logical(kvwt[:, :, seq - win_keep:], (b,))[None]
    win_sample = _token_minor_to_logical(win_new_t, (nb,))[None]
    pool_prompt = z[:, seq - POOL_HIST:][None]
    pool_sample = jnp.concatenate([state_pool[0], zs[:, None, :]], axis=1)[:, 1:][None]
    return (y_p, y_s.reshape(nb, 1, D_MODEL), kv_prompt, kv_sample,
            win_prompt, win_sample, pool_prompt, pool_sample)
```

```python
import functools

import numpy as np
import jax
import jax.numpy as jnp
from jax import lax
from jax.experimental import pallas as pl
from jax.experimental.pallas import tpu as pltpu

F32 = jnp.float32
BF16 = jnp.bfloat16

D_MODEL = 1024
POOL_WIDTH = 256
POOL_WINDOWS = (2, 4, 8, 16)
POOL_GROUP_DIM = 64
POOL_HIST = 15
HEAD_DIM = 64
N_HEADS = 12
N_KV_HEADS = 4
GQA = 3
NSA_WIDTH = 768
KV_WIDTH = 256
CMP_BLOCK = 32
CMP_STRIDE = 16
SEL_BLOCK = 64
SEL_TOP_N = 16
WINDOW = 512
ROPE_DIM = 16
ROPE_THETA = 500000.0
EPS = 1e-6
FORCE_SCORE = 1e4
IN_SPLITS = (256, 256, 768, 768, 1024, 512, 36)
IN_OFFS = tuple(int(v) for v in np.cumsum((0,) + IN_SPLITS))
SCALE = HEAD_DIM ** -0.5

LANES = 128
VMEM_LIMIT_BYTES = 56 * 1024 * 1024

NEG_INF = float("-inf")
N_SLABS = 2 * KV_WIDTH // LANES


def _cparams(semantics):
    return pltpu.CompilerParams(dimension_semantics=semantics, vmem_limit_bytes=VMEM_LIMIT_BYTES)


def _dot(a, b):
    return jnp.dot(a, b, preferred_element_type=F32)


def _dot_nt(a, b):
    return lax.dot_general(a, b, (((1,), (1,)), ((), ())), preferred_element_type=F32)


def _split_dot(x, w_bf16):
    hi = x.astype(BF16)
    lo = (x - hi.astype(F32)).astype(BF16)
    return _dot(hi, w_bf16) + _dot(lo, w_bf16)


def _silu(x):
    return x * jax.nn.sigmoid(x)


def _safe_softmax(s):
    m = jnp.max(s, axis=-1, keepdims=True)
    m = jnp.where(m == NEG_INF, 0.0, m)
    e = jnp.exp(s - m)
    return e / jnp.maximum(jnp.sum(e, axis=-1, keepdims=True), 1e-30)


def _head_norm_rope(v, gain, bd, cos, sin_lo, sin_hi):
    outs = []
    for c in range(v.shape[1] // LANES):
        vc = v[:, c * LANES:(c + 1) * LANES]
        ms = _split_dot(vc * vc, bd)
        n = vc * lax.rsqrt(ms + EPS) * gain[:, c * LANES:(c + 1) * LANES]
        outs.append(n * cos + pltpu.roll(n, LANES - 8, 1) * sin_lo + pltpu.roll(n, 8, 1) * sin_hi)
    return jnp.concatenate(outs, axis=1)


def _head_norm_rope_t(vt, gain_t, cos_t, sin_t):
    outs = []
    for c in range(vt.shape[1] // LANES):
        sl = slice(c * LANES, (c + 1) * LANES)
        v = vt[:, sl].reshape(N_KV_HEADS, HEAD_DIM, LANES)
        ms = jnp.mean(v * v, axis=1, keepdims=True)
        n = v * lax.rsqrt(ms + EPS) * gain_t.reshape(N_KV_HEADS, HEAD_DIM, LANES)
        half = ROPE_DIM // 2
        x1, x2 = n[:, 0:half, :], n[:, half:ROPE_DIM, :]
        cos, sin = cos_t[:, sl][None], sin_t[:, sl][None]
        r = jnp.concatenate([x1 * cos - x2 * sin, x2 * cos + x1 * sin, n[:, ROPE_DIM:, :]], axis=1)
        outs.append(r.reshape(N_KV_HEADS * HEAD_DIM, LANES))
    return jnp.concatenate(outs, axis=1)


def _project_kernel(x_ref, gain_ref, w_ref, wt_ref, qg_ref, kgt_ref, bd_ref, cos_ref, slo_ref, shi_ref, cost_ref, sint_ref,
                    z_ref, sgp_ref, q_ref, sgn_ref, gate_ref, kvt_ref, kvwt_ref, *row_refs):
    x = x_ref[0]
    ms = jnp.mean(x * x, axis=-1, keepdims=True)
    hb = (x * lax.rsqrt(ms + EPS) * gain_ref[...]).astype(BF16)

    def proj(lo, hi):
        return _dot(hb, w_ref[:, lo:hi])

    o = IN_OFFS
    z_ref[0] = proj(o[0], o[1])
    sgp_ref[0] = _silu(proj(o[1], o[2]))
    q_ref[0] = _head_norm_rope(proj(o[2], o[3]), qg_ref[...], bd_ref[...], cos_ref[...], slo_ref[...], shi_ref[...])
    sgn_ref[0] = _silu(proj(o[3], o[4]))
    gate_ref[0] = jax.nn.sigmoid(proj(o[6], o[7]))
    key_gain = {0: 0, 2: 1, 4: 2}
    cos_t, sin_t = cost_ref[...], sint_ref[...]
    for grp in range(6):
        val_t = _dot_nt(wt_ref[grp * KV_WIDTH:(grp + 1) * KV_WIDTH, :], hb)
        if grp in key_gain:
            val_t = _head_norm_rope_t(val_t, kgt_ref[key_gain[grp]], cos_t, sin_t)
        dst, r0 = (kvt_ref, grp * KV_WIDTH) if grp < 4 else (kvwt_ref, (grp - 4) * KV_WIDTH)
        dst[0, r0:r0 + KV_WIDTH, :] = val_t
        if row_refs:
            row_refs[0 if grp < 4 else 1][0, :, r0:r0 + KV_WIDTH] = val_t.T


def _rope_tables(pos):
    half = ROPE_DIM // 2
    freqs = ROPE_THETA ** (-jnp.arange(half, dtype=F32) / half)
    ang = pos.astype(F32)[:, None] * freqs[None, :]
    cos8, sin8 = jnp.cos(ang), jnp.sin(ang)
    p = pos.shape[0]
    rest = HEAD_DIM - ROPE_DIM
    cos = jnp.concatenate([cos8, cos8, jnp.ones((p, rest), F32)], axis=1)
    slo = jnp.concatenate([-sin8, jnp.zeros((p, HEAD_DIM - half), F32)], axis=1)
    shi = jnp.concatenate([jnp.zeros((p, half), F32), sin8, jnp.zeros((p, rest), F32)], axis=1)
    return tuple(jnp.tile(t, (1, LANES // HEAD_DIM)) for t in (cos, slo, shi)) + (cos8.T, sin8.T)


def _project(x, pos, tm, ln_gain, w_bf16, q_norm, k_norm, emit_rows):
    b, s, _ = x.shape
    assert pos.shape[0] == s and tm % LANES == 0
    cos, slo, shi, cos_t, sin_t = _rope_tables(pos)
    tab_spec = pl.BlockSpec((tm, LANES), lambda bb, i: (i, 0))
    tab_t_spec = pl.BlockSpec((ROPE_DIM // 2, tm), lambda bb, i: (0, i))
    bd = jnp.asarray(np.kron(np.eye(LANES // HEAD_DIM), np.full((HEAD_DIM, HEAD_DIM), 1.0 / HEAD_DIM)), BF16)
    qg = jnp.tile(q_norm, N_HEADS)[None, :]
    kgt = jnp.broadcast_to(jnp.tile(k_norm, (1, N_KV_HEADS))[:, :, None], (k_norm.shape[0], KV_WIDTH, LANES))
    gain = ln_gain[None, :]
    wt = w_bf16[:, IN_OFFS[4]:IN_OFFS[6]].T

    def full(a):
        return pl.BlockSpec(a.shape, lambda bb, i: (0,) * a.ndim)

    def rows(w):
        return pl.BlockSpec((1, tm, w), lambda bb, i: (bb, i, 0))

    def cols(f):
        return pl.BlockSpec((1, f, tm), lambda bb, i: (bb, 0, i))

    row_w = (256, 256, 768, 768, 36)
    out_specs = [rows(w) for w in row_w] + [cols(4 * KV_WIDTH), cols(2 * KV_WIDTH)]
    out_shape = ([jax.ShapeDtypeStruct((b, s, w), F32) for w in row_w]
                 + [jax.ShapeDtypeStruct((b, 4 * KV_WIDTH, s), F32), jax.ShapeDtypeStruct((b, 2 * KV_WIDTH, s), F32)])
    if emit_rows:
        out_specs += [rows(4 * KV_WIDTH), rows(2 * KV_WIDTH)]
        out_shape += [jax.ShapeDtypeStruct((b, s, 4 * KV_WIDTH), F32), jax.ShapeDtypeStruct((b, s, 2 * KV_WIDTH), F32)]
    return pl.pallas_call(
        _project_kernel,
        grid=(b, s // tm),
        in_specs=[rows(D_MODEL), full(gain), full(w_bf16), full(wt), full(qg), full(kgt), full(bd),
                  tab_spec, tab_spec, tab_spec, tab_t_spec, tab_t_spec],
        out_specs=out_specs,
        out_shape=out_shape,
        compiler_params=_cparams(("parallel", "parallel")),
        name="project",
    )(x, gain, w_bf16, wt, qg, kgt, bd, cos, slo, shi, cos_t, sin_t)


def _pool_prompt_kernel(z_ref, sgp_ref, wp_ref, scale_ref, out_ref, pad_ref):
    length = z_ref.shape[1]
    pad_ref[0:16, :] = jnp.zeros((16, POOL_WIDTH), F32)
    pad_ref[16:16 + length, :] = z_ref[0]
    z = pad_ref[16:16 + length, :]
    acc = z
    sums = {}
    for k in range(1, POOL_HIST + 1):
        acc = acc + pad_ref[16 - k:16 - k + length, :]
        if k + 1 in POOL_WINDOWS:
            sums[k + 1] = acc
    pos = lax.broadcasted_iota(jnp.int32, (length, 1), 0)
    lane = lax.broadcasted_iota(jnp.int32, (1, POOL_WIDTH), 1)
    means = None
    for gi, w in enumerate(POOL_WINDOWS):
        m = sums[w] / jnp.minimum(w, pos + 1).astype(F32)
        means = m if means is None else jnp.where(lane >= gi * POOL_GROUP_DIM, m, means)
    mixed = _dot((means - z).astype(BF16), wp_ref[...])
    out_ref[0] = mixed * scale_ref[...] * sgp_ref[0]


def _pool_prompt(z, sgp, wp_bd, scale):
    b, length, _ = z.shape
    blk = pl.BlockSpec((1, length, POOL_WIDTH), lambda i: (i, 0, 0))
    return pl.pallas_call(
        _pool_prompt_kernel,
        grid=(b,),
        in_specs=[blk, blk, pl.BlockSpec(wp_bd.shape, lambda i: (0, 0)), pl.BlockSpec(scale.shape, lambda i: (0, 0))],
        out_specs=blk,
        out_shape=jax.ShapeDtypeStruct(z.shape, F32),
        scratch_shapes=[pltpu.VMEM((16 + length, POOL_WIDTH), F32)],
        compiler_params=_cparams(("parallel",)),
        name="pool_prompt",
    )(z, sgp, wp_bd, scale)


def _compress_accumulate(load_rows, w1_ref, slot, nj):
    acc = jnp.zeros((nj, 2 * LANES), F32)
    for s2 in range(CMP_STRIDE // 2):
        lhs = jnp.concatenate([load_rows(2 * s2), load_rows(2 * s2 + 1)], axis=1).astype(BF16)
        acc = acc + _dot(lhs, w1_ref[slot, s2])
    return acc


def _compress_finish(acc, b1, w2):
    nj = acc.shape[0]
    second = pltpu.roll(acc[:, LANES:], nj - 1, 0)
    hid = _silu(acc[:, :LANES] + second + b1)
    return _dot(hid.astype(BF16), w2)


def _compress_prompt_kernel(kvt_ref, w1_ref, b1_ref, w2_ref, kc_ref, vc_ref, xbuf):
    length = kvt_ref.shape[2]
    nj = length // CMP_STRIDE
    for sl in range(N_SLABS):
        for tt in range(length // LANES):
            xbuf[sl, tt * LANES:(tt + 1) * LANES, :] = kvt_ref[0, sl * LANES:(sl + 1) * LANES, tt * LANES:(tt + 1) * LANES].T
    for sl in range(N_SLABS):
        slot, out_ref = sl // 2, (kc_ref, vc_ref)[sl // 2]
        acc = _compress_accumulate(lambda s, sl=sl: xbuf[sl, pl.ds(s, nj, stride=CMP_STRIDE), :], w1_ref, slot, nj)
        out_ref[0, :, (sl % 2) * LANES:(sl % 2 + 1) * LANES] = _compress_finish(acc, b1_ref[slot], w2_ref[slot])


def _compress_prompt(kvt, w1cat, b1t, w2bd):
    b, _, length = kvt.shape
    nj = length // CMP_STRIDE

    def full(a):
        return pl.BlockSpec(a.shape, lambda bb: (0,) * a.ndim)

    out_blk = pl.BlockSpec((1, nj, KV_WIDTH), lambda bb: (bb, 0, 0))
    return pl.pallas_call(
        _compress_prompt_kernel,
        grid=(b,),
        in_specs=[pl.BlockSpec((1, 2 * KV_WIDTH, length), lambda bb: (bb, 0, 0)), full(w1cat), full(b1t), full(w2bd)],
        out_specs=[out_blk, out_blk],
        out_shape=[jax.ShapeDtypeStruct((b, nj, KV_WIDTH), F32)] * 2,
        scratch_shapes=[pltpu.VMEM((N_SLABS, length, LANES), F32)],
        compiler_params=_cparams(("parallel",)),
        name="compress_prompt",
    )(kvt, w1cat, b1t, w2bd)


def _compress_weights(w_cmp1, b_cmp1, w_cmp2):
    eye2 = jnp.eye(LANES // HEAD_DIM, dtype=F32)

    def bd2(m):
        return jnp.kron(eye2, m)

    w1cat = []
    for slot in range(2):
        per = []
        for s2 in range(CMP_STRIDE // 2):
            blocks = []
            for s in (2 * s2, 2 * s2 + 1):
                blocks.append(jnp.concatenate([bd2(w_cmp1[slot, s]), bd2(w_cmp1[slot, CMP_STRIDE + s])], axis=1))
            per.append(jnp.concatenate(blocks, axis=0))
        w1cat.append(jnp.stack(per))
    w1cat = jnp.stack(w1cat).astype(BF16)
    b1t = jnp.tile(b_cmp1, (1, LANES // HEAD_DIM))[:, None, :]
    w2bd = jnp.stack([bd2(w_cmp2[0]), bd2(w_cmp2[1])]).astype(BF16)
    return w1cat, b1t, w2bd


def _overlap_matrix(n_cmp_pad, n_cmp, n_sel, n_sel_pad):
    cstart = np.arange(n_cmp_pad)[:, None] * CMP_STRIDE
    bstart = np.arange(n_sel_pad)[None, :] * SEL_BLOCK
    ov = np.clip(np.minimum(cstart + CMP_BLOCK, bstart + SEL_BLOCK) - np.maximum(cstart, bstart), 0, None) / CMP_BLOCK
    ov = ov * (np.arange(n_cmp_pad)[:, None] < n_cmp) * (np.arange(n_sel_pad)[None, :] < n_sel)
    return jnp.asarray(ov, BF16)


def _topk_mask(score, n_iter):
    lane = lax.broadcasted_iota(jnp.int32, score.shape, 1)
    rank = jnp.zeros(score.shape, F32)
    for j in range(n_iter):
        col = score[:, j:j + 1]
        ahead = (col > score) | ((col == score) & (lane > j))
        rank = rank + jnp.where(ahead, 1.0, 0.0)
    return rank, (rank < SEL_TOP_N) & (score > NEG_INF)


TQ = 256
WIN_TILES = WINDOW // TQ + 1
Q_GROUPS = 4


def _topk_mask_t(score_t):
    row = lax.broadcasted_iota(jnp.int32, score_t.shape, 0)
    rank = jnp.zeros(score_t.shape, F32)
    for j in range(score_t.shape[0]):
        cur = score_t[j:j + 1, :]
        ahead = (cur > score_t) | ((cur == score_t) & (row > j))
        rank = rank + jnp.where(ahead, 1.0, 0.0)
    return (rank < SEL_TOP_N) & (score_t > NEG_INF)


def _attend_post(s, v_t, half):
    m = jnp.max(s, axis=-1, keepdims=True)
    m = jnp.where(m == NEG_INF, 0.0, m)
    e = jnp.exp(s - m)
    row_half = lax.broadcasted_iota(jnp.int32, (v_t.shape[0], 1), 0) // HEAD_DIM
    out = _dot_nt(e.astype(BF16), jnp.where(row_half == half, v_t, jnp.ones_like(v_t)))
    return out / jnp.maximum(pltpu.roll(out, HEAD_DIM, 1), 1e-30)


def _nsa_prompt_kernel(*refs, tile0):
    q_ref, gate_ref, kc_ref, vc_ref, kst_ref, vst_ref = refs[:6]
    win_refs, (ovt_ref, ex_ref, out_ref) = refs[6:6 + WIN_TILES], refs[6 + WIN_TILES:]
    i = tile0 + pl.program_id(1)
    nk = kst_ref.shape[2]
    n_cmp = kc_ref.shape[1]
    n_sel = ex_ref.shape[0]
    qpos = i * TQ + lax.broadcasted_iota(jnp.int32, (TQ, 1), 0)
    qpos_l = i * TQ + lax.broadcasted_iota(jnp.int32, (1, TQ), 1)
    lane = lax.broadcasted_iota(jnp.int32, (1, LANES), 1)
    lane_half = lane // HEAD_DIM

    ncol = lax.broadcasted_iota(jnp.int32, (1, n_cmp), 1)
    bias_c = jnp.where(ncol * CMP_STRIDE + CMP_BLOCK - 1 <= qpos, 0.0, NEG_INF)
    bias_w = []
    for u in range(WIN_TILES):
        kb = i - (WIN_TILES - 1) + u
        rel = qpos - (kb * TQ + lax.broadcasted_iota(jnp.int32, (1, TQ), 1))
        bias_w.append(jnp.where((kb >= 0) & (rel >= 0) & (rel < WINDOW), 0.0, NEG_INF))
    bias_w = jnp.concatenate(bias_w, axis=1)
    jrow = lax.broadcasted_iota(jnp.int32, (n_sel, 1), 0)
    qblk_l = qpos_l // SEL_BLOCK
    valid_t = jrow * SEL_BLOCK <= qpos_l
    forced_t = (jrow == 0) | (jrow == qblk_l) | (jrow == qblk_l - 1)
    causal = lax.broadcasted_iota(jnp.int32, (1, nk), 1) <= qpos
    gates = gate_ref[0]

    def add_bias(s, bias):
        return (s.reshape(GQA, TQ, s.shape[-1]) + bias[None]).reshape(GQA * TQ, s.shape[-1])

    heads = range(N_KV_HEADS)
    span = [((h // 2) * LANES, (h // 2 + 1) * LANES) for h in heads]

    qzs = []
    for h in heads:
        rows = []
        for g in range(GQA):
            qh = h * GQA + g
            x = q_ref[0, :, (qh // 2) * LANES:(qh // 2 + 1) * LANES] * SCALE
            if qh % 2 != h % 2:
                x = pltpu.roll(x, HEAD_DIM, 1)
            rows.append(jnp.where(lane_half == h % 2, x, 0.0))
        qzs.append(jnp.concatenate(rows, axis=0).astype(BF16))

    o_cmps, score_ts = [], []
    for h in heads:
        lo, hi = span[h]
        p_c = _safe_softmax(add_bias(_dot_nt(qzs[h], kc_ref[0, :, lo:hi].astype(BF16)), bias_c))
        o_cmps.append(_dot(p_c.astype(BF16), vc_ref[0, :, lo:hi].astype(BF16)))
        p_sum = p_c[0:TQ] + p_c[TQ:2 * TQ] + p_c[2 * TQ:3 * TQ]
        p_hi = p_sum.astype(BF16)
        p_lo = (p_sum - p_hi.astype(F32)).astype(BF16)
        imp_t = _dot_nt(ovt_ref[...], p_hi) + _dot_nt(ovt_ref[...], p_lo)
        score_ts.append(jnp.where(valid_t, jnp.where(forced_t, FORCE_SCORE, imp_t), NEG_INF))

    sels, o_wins = [], []
    for h in heads:
        lo, hi = span[h]
        sels.append(jnp.where(_topk_mask_t(score_ts[h]), 1.0, 0.0).T.astype(BF16))
        kwt = jnp.concatenate([w[0, lo:hi, :] for w in win_refs], axis=1).astype(BF16)
        vwt = jnp.concatenate([w[0, KV_WIDTH + lo:KV_WIDTH + hi, :] for w in win_refs], axis=1).astype(BF16)
        o_wins.append(_attend_post(add_bias(_dot(qzs[h], kwt), bias_w), vwt, h % 2))

    o_slcs = []
    for h in heads:
        lo, hi = span[h]
        selw = _dot(sels[h], ex_ref[...])
        bias_s = jnp.where((selw > 0.5) & causal, 0.0, NEG_INF)
        o_slcs.append(_attend_post(add_bias(_dot(qzs[h], kst_ref[0, lo:hi, :].astype(BF16)), bias_s),
                                   vst_ref[0, lo:hi, :].astype(BF16), h % 2))

    out_chunks = [jnp.zeros((TQ, LANES), F32) for _ in range(NSA_WIDTH // LANES)]
    for h in heads:
        half, o_cmp, o_slc, o_win = h % 2, o_cmps[h], o_slcs[h], o_wins[h]
        for g in range(GQA):
            qh = h * GQA + g
            gi = qh * 3
            sl = slice(g * TQ, (g + 1) * TQ)
            o = (gates[:, gi:gi + 1] * o_cmp[sl] + gates[:, gi + 1:gi + 2] * o_slc[sl] + gates[:, gi + 2:gi + 3] * o_win[sl])
            if qh % 2 != half:
                o = pltpu.roll(o, HEAD_DIM, 1)
            out_chunks[qh // 2] = out_chunks[qh // 2] + jnp.where(lane_half == qh % 2, o, 0.0)
    for c, val in enumerate(out_chunks):
        out_ref[0, :, c * LANES:(c + 1) * LANES] = val


def _nsa_prompt(q, gates, kc, vc, kvt, kvwt):
    b, seq, _ = q.shape
    n_cmp = kc.shape[1]
    n_sel = seq // SEL_BLOCK
    n_tiles = seq // TQ
    assert n_sel % 8 == 0 and n_tiles % Q_GROUPS == 0 and (seq // Q_GROUPS) % LANES == 0
    ovt = _overlap_matrix(n_cmp, (seq - CMP_BLOCK) // CMP_STRIDE + 1, n_sel, n_sel).T
    ex = np.arange(n_sel)[:, None] == (np.arange(seq)[None, :] // SEL_BLOCK)
    tiles_per_group = n_tiles // Q_GROUPS

    outs = []
    for grp in range(Q_GROUPS):
        tile0 = grp * tiles_per_group
        nk = (grp + 1) * (seq // Q_GROUPS)
        ex_g = jnp.asarray(ex[:, :nk], BF16)

        def tile(w, tile0=tile0):
            return pl.BlockSpec((1, TQ, w), lambda bb, i: (bb, tile0 + i, 0))

        def whole(rows):
            return pl.BlockSpec((1, rows, KV_WIDTH), lambda bb, i: (bb, 0, 0))

        def feat_rows(row_blk, nk=nk):
            return pl.BlockSpec((1, KV_WIDTH, nk), lambda bb, i, rb=row_blk: (bb, rb, 0))

        def win_tile(u, tile0=tile0):
            return pl.BlockSpec((1, 2 * KV_WIDTH, TQ),
                                lambda bb, i, u=u: (bb, 0, jnp.maximum(tile0 + i - (WIN_TILES - 1) + u, 0)))

        def full(a):
            return pl.BlockSpec(a.shape, lambda bb, i: (0,) * a.ndim)

        in_specs = ([tile(NSA_WIDTH), tile(gates.shape[-1]), whole(n_cmp), whole(n_cmp), feat_rows(2), feat_rows(3)]
                    + [win_tile(u) for u in range(WIN_TILES)] + [full(ovt), full(ex_g)])
        args = [q, gates, kc, vc, kvt, kvt] + [kvwt] * WIN_TILES + [ovt, ex_g]
        outs.append(pl.pallas_call(
            functools.partial(_nsa_prompt_kernel, tile0=tile0),
            grid=(b, tiles_per_group),
            in_specs=in_specs,
            out_specs=pl.BlockSpec((1, TQ, NSA_WIDTH), lambda bb, i: (bb, i, 0)),
            out_shape=jax.ShapeDtypeStruct((b, seq // Q_GROUPS, NSA_WIDTH), F32),
            compiler_params=_cparams(("parallel", "arbitrary")),
            name=f"nsa_prompt_g{grp}",
        )(*args))
    return outs


def _out_proj_kernel(x_ref, gp_ref, o_ref, sgn_ref, w_ref, y_ref):
    gn = (o_ref[...] * sgn_ref[...]).astype(BF16)
    y_ref[...] = (x_ref[...] + _dot(gp_ref[...].astype(BF16), w_ref[0:POOL_WIDTH, :])
                  + _dot(gn, w_ref[POOL_WIDTH:, :]))


def _out_proj_groups_kernel(x_ref, gp_ref, sgn_ref, w_ref, *rest):
    o_refs, y_ref = rest[:-1], rest[-1]
    g = pl.program_id(1)
    o = o_refs[0][0]
    for k in range(1, len(o_refs)):
        o = jnp.where(g == k, o_refs[k][0], o)
    gn = (o * sgn_ref[0]).astype(BF16)
    y_ref[0] = (x_ref[0] + _dot(gp_ref[0].astype(BF16), w_ref[0:POOL_WIDTH, :]) + _dot(gn, w_ref[POOL_WIDTH:, :]))


def _out_proj_groups(x, gp, o_groups, sgn, w_bf16):
    b, seq, _ = x.shape
    n_groups = len(o_groups)
    tm = seq // n_groups

    def rows(w):
        return pl.BlockSpec((1, tm, w), lambda bb, g: (bb, g, 0))

    grp = pl.BlockSpec((1, tm, NSA_WIDTH), lambda bb, g: (bb, 0, 0))
    return pl.pallas_call(
        _out_proj_groups_kernel,
        grid=(b, n_groups),
        in_specs=[rows(D_MODEL), rows(POOL_WIDTH), rows(NSA_WIDTH), pl.BlockSpec(w_bf16.shape, lambda bb, g: (0, 0))]
        + [grp] * n_groups,
        out_specs=rows(D_MODEL),
        out_shape=jax.ShapeDtypeStruct((b, seq, D_MODEL), F32),
        compiler_params=_cparams(("parallel", "arbitrary")),
        name="out_proj_prompt",
    )(x, gp, sgn, w_bf16, *o_groups)


def _out_proj(x, gp, o_nsa, sgn, w_bf16, tm):
    t = x.shape[0]

    def rows(w):
        return pl.BlockSpec((tm, w), lambda i: (i, 0))

    return pl.pallas_call(
        _out_proj_kernel,
        grid=(t // tm,),
        in_specs=[rows(D_MODEL), rows(POOL_WIDTH), rows(NSA_WIDTH), rows(NSA_WIDTH),
                  pl.BlockSpec(w_bf16.shape, lambda i: (0, 0))],
        out_specs=rows(D_MODEL),
        out_shape=jax.ShapeDtypeStruct((t, D_MODEL), F32),
        compiler_params=_cparams(("parallel",)),
        name="out_proj",
    )(x, gp, o_nsa, sgn, w_bf16)


ROWS = GQA * 8
CH_PAGES = 32


def _page_copy(cache_ref, xt, sem, page, p, slot):
    return pltpu.make_async_copy(cache_ref.at[page, pl.ds(0, 2 * KV_WIDTH), :], xt.at[slot, p], sem.at[slot])


def _compress_decode_kernel(pt_ref, cache_ref, qbd_ref, perm_ref, w1_ref, b1_ref, w2_ref, ov_ref, ocmp_ref, idx_ref,
                            xt, xs, fg, sem, *, n_pages, page_size, n_chunks, nch):
    t = pl.program_id(0)
    bpp = page_size // CMP_STRIDE
    jch = CH_PAGES * bpp
    nj = nch * jch
    past = n_pages * page_size

    def chunk_dma(step, slot, wait):
        base = (step // nch) * n_pages + (step % nch) * CH_PAGES

        def body(p, carry):
            cp = _page_copy(cache_ref, xt, sem, pt_ref[base + p], p, slot)
            cp.wait() if wait else cp.start()
            return carry
        lax.fori_loop(0, CH_PAGES, body, 0, unroll=8)

    @pl.when(t == 0)
    def _():
        chunk_dma(0, 0, wait=False)
        xs[...] = jnp.zeros(xs.shape, BF16)

    @pl.when(t + 1 < n_chunks)
    def _():
        chunk_dma(t + 1, (t + 1) % 2, wait=False)

    slot = t % 2

    @pl.when(t < n_chunks)
    def _():
        chunk_dma(t, slot, wait=True)

    def stage():
        for pp in range(CH_PAGES // 2):
            ys = [_dot_nt(perm_ref[...], xt[slot, 2 * pp + k].astype(BF16)) for k in range(2)]
            for s in range(CMP_STRIDE):
                rows = jnp.concatenate([y[s * bpp:(s + 1) * bpp, :] for y in ys], axis=0)
                xs[slot, s, pp * 2 * bpp:(pp + 1) * 2 * bpp, :] = rows.astype(BF16)

    prev = 1 - slot
    row0 = pl.multiple_of(((t + nch - 1) % nch) * jch, jch)
    for sl in range(N_SLABS):
        acc = jnp.zeros((jch, 2 * LANES), F32)
        for s2 in range(CMP_STRIDE // 2):
            lhs = jnp.concatenate([xs[prev, 2 * s2, :, sl * LANES:(sl + 1) * LANES],
                                   xs[prev, 2 * s2 + 1, :, sl * LANES:(sl + 1) * LANES]], axis=1)
            acc = acc + _dot(lhs, w1_ref[sl // 2, s2])
        fg[sl, pl.ds(row0, jch), :] = acc

    def finish():
        kcv = [_compress_finish(fg[sl], b1_ref[sl // 2], w2_ref[sl // 2]) for sl in range(N_SLABS)]
        kc = jnp.concatenate(kcv[0:2], axis=1).astype(BF16)
        vc = jnp.concatenate(kcv[2:4], axis=1).astype(BF16)
        s = _dot_nt(qbd_ref[0].astype(BF16), kc)
        ncol = lax.broadcasted_iota(jnp.int32, (1, nj), 1)
        s = s + jnp.where(ncol * CMP_STRIDE + CMP_BLOCK - 1 <= past, 0.0, NEG_INF)
        p = _safe_softmax(s)
        ocmp_ref[0] = _dot(p.astype(BF16), vc)
        p_sum = p[0:8] + p[8:16] + p[16:24]
        imp = _split_dot(p_sum, ov_ref[...])
        lane = lax.broadcasted_iota(jnp.int32, (1, imp.shape[1]), 1)
        qblk = past // SEL_BLOCK
        forced = (lane == 0) | (lane == qblk) | (lane == qblk - 1)
        score = jnp.where(lane * SEL_BLOCK <= past, jnp.where(forced, FORCE_SCORE, imp), NEG_INF)
        rank, sel = _topk_mask(score, qblk + 1)
        lanef = lane.astype(F32)
        lane16 = lax.broadcasted_iota(jnp.int32, (1, SEL_TOP_N), 1)
        idx = jnp.zeros((8, SEL_TOP_N), F32)
        for r in range(SEL_TOP_N):
            v = jnp.sum(jnp.where((rank == r) & sel, lanef, 0.0), axis=-1, keepdims=True)
            idx = jnp.where(lane16 == r, v, idx)
        idx_ref[0] = idx.astype(jnp.int32)

    stage()
    pl.when((t > 0) & (t % nch == 0))(finish)


def _compress_decode(page_table, cache_t, qbd, w1cat, b1t, w2bd):
    nb, n_pages = page_table.shape
    page_size = cache_t.shape[2]
    assert n_pages % CH_PAGES == 0 and page_size % LANES == 0
    nch = n_pages // CH_PAGES
    past = n_pages * page_size
    nj = past // CMP_STRIDE
    n_sel = past // SEL_BLOCK + 1
    assert n_sel > SEL_TOP_N
    n_sel_pad = -(-n_sel // LANES) * LANES
    ov = _overlap_matrix(nj, (past + 1 - CMP_BLOCK) // CMP_STRIDE + 1, n_sel, n_sel_pad)
    bpp = page_size // CMP_STRIDE
    tok = np.arange(page_size)
    perm = jnp.asarray(((tok % CMP_STRIDE) * bpp + tok // CMP_STRIDE)[None, :] == np.arange(page_size)[:, None], BF16)

    def full(a):
        return pl.BlockSpec(a.shape, lambda t, pt: (0,) * a.ndim)

    def per_b(shape):
        return pl.BlockSpec((1,) + shape, lambda t, pt: (jnp.maximum(t - 1, 0) // nch, 0, 0))

    n_chunks = nb * nch
    grid_spec = pltpu.PrefetchScalarGridSpec(
        num_scalar_prefetch=1,
        grid=(n_chunks + 1,),
        in_specs=[pl.BlockSpec(memory_space=pl.ANY), per_b((ROWS, KV_WIDTH)), full(perm), full(w1cat), full(b1t),
                  full(w2bd), full(ov)],
        out_specs=[per_b((ROWS, KV_WIDTH)), per_b((8, SEL_TOP_N))],
        scratch_shapes=[pltpu.VMEM((2, CH_PAGES, 2 * KV_WIDTH, page_size), F32),
                        pltpu.VMEM((2, CMP_STRIDE, CH_PAGES * bpp, 2 * KV_WIDTH), BF16),
                        pltpu.VMEM((N_SLABS, nj, 2 * LANES), F32),
                        pltpu.SemaphoreType.DMA((2,))],
    )
    return pl.pallas_call(
        functools.partial(_compress_decode_kernel, n_pages=n_pages, page_size=page_size, n_chunks=n_chunks, nch=nch),
        grid_spec=grid_spec,
        out_shape=[jax.ShapeDtypeStruct((nb, ROWS, KV_WIDTH), F32), jax.ShapeDtypeStruct((nb, 8, SEL_TOP_N), jnp.int32)],
        compiler_params=_cparams(("arbitrary",)),
        name="compress_decode",
    )(page_table.reshape(-1), cache_t, qbd, perm, w1cat, b1t, w2bd, ov)


def _block_copy(cache_ref, kvbuf, sem, pt_ref, idx_ref, bb, h, r, slot, n_pages, page_size):
    bpp = page_size // SEL_BLOCK
    j = idx_ref[(bb * N_KV_HEADS + h) * SEL_TOP_N + r]
    jj = jnp.minimum(j, n_pages * bpp - 1)
    page = pt_ref[bb * n_pages + jj // bpp]
    dst0 = pl.multiple_of(r * page_size, page_size)
    return pltpu.make_async_copy(cache_ref.at[page, pl.ds(2, 2), h],
                                 kvbuf.at[slot, h, :, :, pl.ds(dst0, page_size)], sem.at[slot])


def _heads_to_lanes64(x, rowh):
    out = jnp.zeros((x.shape[0], HEAD_DIM), F32)
    for h in range(N_KV_HEADS):
        out = jnp.where(rowh == h, x[:, h * HEAD_DIM:(h + 1) * HEAD_DIM], out)
    return out


def _nsa_decode_kernel(pt_ref, idx_ref, cache_ref, qbd_ref, q4_ref, kvn_ref, kvwn_ref, kvwnt_ref, wint_ref, gate_ref,
                       ocmp_ref, idxv_ref, ex_ref, out_ref, winout_ref, kvbuf, sem, *, n_pages, page_size):
    b = pl.program_id(0)
    nb = pl.num_programs(0)
    bpp = page_size // SEL_BLOCK
    n_past_blocks = n_pages * bpp
    nk = SEL_TOP_N * page_size
    win_buf = wint_ref.shape[2]

    def gather(bb, slot, wait):
        for h in range(N_KV_HEADS):
            def body(r, carry, h=h):
                cp = _block_copy(cache_ref, kvbuf, sem, pt_ref, idx_ref, bb, h, r, slot, n_pages, page_size)
                cp.wait() if wait else cp.start()
                return carry
            lax.fori_loop(0, SEL_TOP_N, body, 0)

    @pl.when(b == 0)
    def _():
        gather(0, 0, wait=False)

    nxt, nslot = (b + 1) % nb, (b + 1) % 2
    slot = b % 2
    gather(b, slot, wait=True)

    qbd = qbd_ref[0]
    rowh = lax.broadcasted_iota(jnp.int32, (ROWS, 1), 0) % 8
    kvn = kvn_ref[0]
    k_new, v_new = kvn[:, 2 * KV_WIDTH:3 * KV_WIDTH], kvn[:, 3 * KV_WIDTH:]
    jcol = _dot(idxv_ref[0].astype(F32).astype(BF16), ex_ref[...]).astype(jnp.int32)
    col_blk = (lax.broadcasted_iota(jnp.int32, (1, nk), 1) % page_size) // SEL_BLOCK
    in_cache = jcol < n_past_blocks
    bias_all = jnp.where(in_cache & (col_blk == jcol % bpp), 0.0, NEG_INF)
    new_bias = jnp.where(jnp.max(jcol, axis=-1, keepdims=True) >= n_past_blocks, 0.0, NEG_INF)
    new_bias = jnp.concatenate([new_bias] * GQA, axis=0)

    s_new = jnp.sum(qbd * k_new, axis=-1, keepdims=True)
    o_slc = jnp.zeros((ROWS, HEAD_DIM), F32)
    for h in range(N_KV_HEADS):
        s = _dot(q4_ref[0, h].astype(BF16), kvbuf[slot, h, 0].astype(BF16)) + bias_all[h:h + 1, :]
        sn = s_new + new_bias
        m = jnp.maximum(jnp.max(s, axis=-1, keepdims=True), sn)
        m = jnp.where(m == NEG_INF, 0.0, m)
        e, en = jnp.exp(s - m), jnp.exp(sn - m)
        den = jnp.maximum(jnp.sum(e, axis=-1, keepdims=True) + en, 1e-30)
        pv = _dot_nt(e.astype(BF16), kvbuf[slot, h, 1].astype(BF16))
        o_h = (pv + en * v_new[:, h * HEAD_DIM:(h + 1) * HEAD_DIM]) / den
        o_slc = jnp.where(rowh == h, o_h, o_slc)
        for r in range(SEL_TOP_N):
            _block_copy(cache_ref, kvbuf, sem, pt_ref, idx_ref, nxt, h, r, nslot, n_pages, page_size).start()

    kwn = kvwn_ref[0]
    s = _dot(qbd.astype(BF16), wint_ref[0, 0:KV_WIDTH, :].astype(BF16))
    rel = win_buf - lax.broadcasted_iota(jnp.int32, (1, win_buf), 1)
    s = s + jnp.where((rel >= 0) & (rel < WINDOW), 0.0, NEG_INF)
    sn = jnp.sum(qbd * kwn[:, :KV_WIDTH], axis=-1, keepdims=True)
    m = jnp.maximum(jnp.max(s, axis=-1, keepdims=True), sn)
    e, en = jnp.exp(s - m), jnp.exp(sn - m)
    den = jnp.sum(e, axis=-1, keepdims=True) + en
    o_w = (_dot_nt(e.astype(BF16), wint_ref[0, KV_WIDTH:, :].astype(BF16)) + en * kwn[:, KV_WIDTH:]) / den

    gates = gate_ref[0]
    out_ref[0] = (gates[:, 0:1] * _heads_to_lanes64(ocmp_ref[0], rowh) + gates[:, 1:2] * o_slc
                  + gates[:, 2:3] * _heads_to_lanes64(o_w, rowh))

    seq_lane = lax.broadcasted_iota(jnp.int32, (1, kvwnt_ref.shape[1]), 1)
    new_col = jnp.sum(jnp.where(seq_lane == b, kvwnt_ref[...], 0.0), axis=-1, keepdims=True)
    row_lane = lax.broadcasted_iota(jnp.int32, (1, win_buf), 1)
    winout_ref[0] = jnp.where(row_lane == win_buf - 1, new_col, pltpu.roll(wint_ref[0], win_buf - 1, 1))

    @pl.when(b == nb - 1)
    def _():
        gather(nxt, nslot, wait=True)


def _nsa_decode(page_table, idx8, cache_t, qbd, q4, kvs, kvws, kvws_t, win_t, gate_rows, ocmp):
    nb, n_pages = page_table.shape
    idx = idx8[:, :N_KV_HEADS, :]
    page_size = cache_t.shape[2]
    assert page_size % SEL_BLOCK == 0 and page_size % LANES == 0
    cache5 = cache_t.reshape(cache_t.shape[0], 4, N_KV_HEADS, HEAD_DIM, page_size)

    def per_b(shape):
        return pl.BlockSpec((1,) + shape, lambda bb, pt, ix: (bb,) + (0,) * len(shape))

    nk = SEL_TOP_N * page_size
    ex = jnp.asarray(np.arange(SEL_TOP_N)[:, None] == (np.arange(nk)[None, :] // page_size), BF16)
    grid_spec = pltpu.PrefetchScalarGridSpec(
        num_scalar_prefetch=2,
        grid=(nb,),
        in_specs=[pl.BlockSpec(memory_space=pl.ANY), per_b((ROWS, KV_WIDTH)), per_b((N_KV_HEADS, ROWS, HEAD_DIM)),
                  per_b((1, 4 * KV_WIDTH)), per_b((1, 2 * KV_WIDTH)),
                  pl.BlockSpec(kvws_t.shape, lambda bb, pt, ix: (0, 0)), per_b(win_t.shape[1:]), per_b((ROWS, 3)),
                  per_b((ROWS, KV_WIDTH)), per_b((8, SEL_TOP_N)), pl.BlockSpec(ex.shape, lambda bb, pt, ix: (0, 0))],
        out_specs=[per_b((ROWS, HEAD_DIM)), per_b(win_t.shape[1:])],
        scratch_shapes=[pltpu.VMEM((2, N_KV_HEADS, 2, HEAD_DIM, nk), F32),
                        pltpu.SemaphoreType.DMA((2,))],
    )
    return pl.pallas_call(
        functools.partial(_nsa_decode_kernel, n_pages=n_pages, page_size=page_size),
        grid_spec=grid_spec,
        out_shape=[jax.ShapeDtypeStruct((nb, ROWS, HEAD_DIM), F32), jax.ShapeDtypeStruct(win_t.shape, F32)],
        compiler_params=_cparams(("arbitrary",)),
        name="nsa_decode",
    )(page_table.reshape(-1), idx.reshape(-1), cache5, qbd, q4, kvs[:, None, :], kvws[:, None, :], kvws_t, win_t,
      gate_rows, ocmp, idx8, ex)


def _pool_sample_kernel(hist_ref, z_ref, sgp_ref, wp_ref, scale_ref, out_ref, *, pos):
    z = z_ref[...]
    acc = z
    sums = {}
    for k in range(1, POOL_HIST + 1):
        acc = acc + hist_ref[POOL_HIST - k]
        if k + 1 in POOL_WINDOWS:
            sums[k + 1] = acc
    lane = lax.broadcasted_iota(jnp.int32, (1, POOL_WIDTH), 1)
    means = None
    for gi, w in enumerate(POOL_WINDOWS):
        m = sums[w] / float(min(w, pos + 1))
        means = m if means is None else jnp.where(lane >= gi * POOL_GROUP_DIM, m, means)
    mixed = _dot((means - z).astype(BF16), wp_ref[...])
    out_ref[...] = mixed * scale_ref[...] * sgp_ref[...]


def _pool_sample(hist_t, z, sgp, wp_bd, scale, pos):
    return pl.pallas_call(
        functools.partial(_pool_sample_kernel, pos=pos),
        out_shape=jax.ShapeDtypeStruct(z.shape, F32),
        name="pool_sample",
    )(hist_t, z, sgp, wp_bd, scale)


def _prep_weights(w_in, w_cmp1, b_cmp1, w_cmp2, w_pool, pool_scale, w_out):
    w1cat, b1t, w2bd = _compress_weights(w_cmp1, b_cmp1, w_cmp2)
    wp_bd = jax.scipy.linalg.block_diag(*[w_pool[g] for g in range(len(POOL_WINDOWS))]).astype(BF16)
    return dict(w_in=w_in.astype(BF16), w_out=w_out.astype(BF16), w1cat=w1cat, b1t=b1t, w2bd=w2bd,
                wp_bd=wp_bd, pool_scale=pool_scale[None, :])


def _rows_layout(x4):
    x = jnp.swapaxes(x4, 1, 2)
    pad = [(0, 0)] * x.ndim
    pad[2] = (0, 8 - N_KV_HEADS)
    x = jnp.pad(x, pad)
    return x.reshape((x.shape[0], ROWS) + x.shape[3:])


def _token_minor_to_logical(xt, lead):
    t = xt.shape[-1]
    x = xt.reshape(lead + (-1, N_KV_HEADS, HEAD_DIM, t))
    n = len(lead)
    return jnp.transpose(x, tuple(range(n)) + (n + 3, n, n + 1, n + 2))


def kernel(x_prompt, x_sample, cache_kv, state_kv_win, state_pool, page_table, ln_gain, w_in, q_norm, k_norm, w_cmp1, b_cmp1, w_cmp2, w_pool, pool_scale, w_out):
    assert cache_kv.shape[0] == 1 and x_sample.shape[1] == 1
    b, seq, _ = x_prompt.shape
    nb = x_sample.shape[0]
    n_pages, page_size = page_table.shape[1], cache_kv.shape[2]
    past = n_pages * page_size
    win_buf = state_kv_win.shape[2]
    w = _prep_weights(w_in[0], w_cmp1[0], b_cmp1[0], w_cmp2[0], w_pool[0], pool_scale[0], w_out[0])

    z, sgp, q, sgn, gates, kvt, kvwt = _project(x_prompt, jnp.arange(seq), 512, ln_gain[0], w["w_in"], q_norm[0],
                                                 k_norm[0], emit_rows=False)
    gp = _pool_prompt(z, sgp, w["wp_bd"], w["pool_scale"])
    kc, vc = _compress_prompt(kvt, w["w1cat"], w["b1t"], w["w2bd"])
    o_groups = _nsa_prompt(q, gates, kc, vc, kvt, kvwt)
    y_p = _out_proj_groups(x_prompt, gp, o_groups, sgn, w["w_out"])

    xs = x_sample.reshape(1, nb, D_MODEL)
    zs, sgps, qs, sgns, gates_s, kvst, kvwst, kvs, kvws = [
        a[0] for a in _project(xs, jnp.full((nb,), past), nb, ln_gain[0], w["w_in"], q_norm[0], k_norm[0], emit_rows=True)]
    cache_t = jnp.transpose(cache_kv[0], (0, 2, 3, 4, 1)).reshape(cache_kv.shape[1], 4 * KV_WIDTH, page_size)
    win_t = jnp.transpose(state_kv_win[0], (0, 2, 3, 4, 1)).reshape(nb, 2 * KV_WIDTH, win_buf)
    q4 = qs.reshape(nb, N_KV_HEADS, GQA, HEAD_DIM) * SCALE
    eye = jnp.eye(N_KV_HEADS, dtype=F32)
    qbd = _rows_layout(jnp.einsum("bhgd,hk->bhgkd", q4, eye).reshape(nb, N_KV_HEADS, GQA, KV_WIDTH))
    q4h = jnp.swapaxes(_rows_layout(jnp.einsum("bhgd,hk->bhgkd", q4, eye)), 1, 2)
    ocmp, idx = _compress_decode(page_table, cache_t, qbd, w["w1cat"], w["b1t"], w["w2bd"])
    gate_rows = _rows_layout(gates_s.reshape(nb, N_KV_HEADS, GQA, 3))
    o_rows, win_new_t = _nsa_decode(page_table, idx, cache_t, qbd, q4h, kvs, kvws, kvwst, win_t, gate_rows, ocmp)
    o_s = jnp.swapaxes(o_rows.reshape(nb, GQA, 8, HEAD_DIM)[:, :, :N_KV_HEADS], 1, 2).reshape(nb, NSA_WIDTH)
    gps = _pool_sample(jnp.swapaxes(state_pool[0], 0, 1), zs, sgps, w["wp_bd"], w["pool_scale"], past)
    y_s = _out_proj(x_sample.reshape(nb, D_MODEL), gps, o_s, sgns, w["w_out"], nb)

    kv_prompt = _token_minor_to_logical(kvt, (b,))[None]
    kv_sample = _token_minor_to_logical(kvst, ())[None, :, None]
    win_keep = min(WINDOW, seq)
    win_prompt = _token_minor_to_logical(kvwt[:, :, seq - win_keep:], (b,))[None]
    win_sample = _token_minor_to_logical(win_new_t, (nb,))[None]
    pool_prompt = z[:, seq - POOL_HIST:][None]
    pool_sample = jnp.concatenate([state_pool[0], zs[:, None, :]], axis=1)[:, 1:][None]
    return (y_p, y_s.reshape(nb, 1, D_MODEL), kv_prompt, kv_sample,
            win_prompt, win_sample, pool_prompt, pool_sample)
```
